```python
import math
import jax
import jax.numpy as jnp
from jax import lax
import numpy as np

D_MODEL = 2048
BATCH = 4
SEQ = 2048
DEPTH = 2
DEC_BATCH = 128
DEC_SEQ = 1
PAST_LEN = 2048
PAGE_SIZE = 128

D_CONV = D_MODEL // 2
CONV_WIDTH = 3
D_SSM = D_MODEL // 2
SSM_GROUP = 16
N_SSM_GROUPS = D_SSM // SSM_GROUP
SSM_STATE = 64
N_DIFF_HEADS = D_MODEL // 128
DIFF_HEAD_DIM = 64
DIFF_V_DIM = 2 * DIFF_HEAD_DIM
D_ATTN = N_DIFF_HEADS * DIFF_V_DIM
Q_BLOCK = 128
N_MEM = 256
N_CROSS_HEADS = 4
CROSS_HEAD_DIM = 128
D_CROSS = N_CROSS_HEADS * CROSS_HEAD_DIM
N_GROUPS = 4
EXPERTS_PER_GROUP = 8
N_EXPERTS = N_GROUPS * EXPERTS_PER_GROUP
TOP_K = 2
D_EXPERT = D_MODEL // 2
MOE_BLOCK = 128
N_REC_LAYERS = (DEPTH + 1) // 2
N_ATTN_LAYERS = DEPTH // 2
EPS = 1e-6

kernel_name = 'hybrid_conv_s5_diffattn_hmoe_step'


def _rmsnorm(x, g):
    xf = x.astype(jnp.float32)
    y = xf * lax.rsqrt(jnp.mean(xf * xf, axis=-1, keepdims=True) + EPS)
    return (y * g.astype(jnp.float32)).astype(x.dtype)


def _short_conv(v, c_gate, b_gate, conv_w, buf):
    u = c_gate * v
    full = jnp.concatenate([buf.astype(u.dtype), u], axis=1)
    t = u.shape[1]
    y = conv_w[0] * full[:, 0:t]
    for j in range(1, CONV_WIDTH):
        y = y + conv_w[j] * full[:, j:j + t]
    return b_gate * y, full[:, t:]


def _linear_combine(left, right):
    a1, b1 = left
    a2, b2 = right
    return a1 * a2, a2 * b1 + b2


def _s5(u, lam_re, lam_im, log_dt, b_re, b_im, c_re, c_im, d_skip, w_glu, h0_re, h0_im):
    f32 = jnp.float32
    bsz, t, _ = u.shape
    uf = u.astype(f32).reshape(bsz, t, N_SSM_GROUPS, SSM_GROUP)
    lam = lax.complex(lam_re.astype(f32), lam_im.astype(f32))
    dt = jnp.exp(log_dt.astype(f32))[:, None]
    lam_bar = jnp.exp(lam * dt)
    b_bar = ((lam_bar - 1.0) / lam)[..., None] * lax.complex(b_re.astype(f32), b_im.astype(f32))
    bu = jnp.einsum('btgc,gnc->btgn', uf.astype(jnp.complex64), b_bar)
    a = jnp.broadcast_to(lam_bar, bu.shape)
    a_cum, h = lax.associative_scan(_linear_combine, (a, bu), axis=1)
    h = h + a_cum * lax.complex(h0_re.astype(f32), h0_im.astype(f32))[:, None]
    c = lax.complex(c_re.astype(f32), c_im.astype(f32))
    y = jnp.real(jnp.einsum('btgn,gcn->btgc', h, c)) + d_skip.astype(f32).reshape(N_SSM_GROUPS, SSM_GROUP) * uf
    y = y.reshape(bsz, t, D_SSM)
    s = jax.nn.gelu(y)
    out = s * jax.nn.sigmoid(s @ w_glu.astype(f32))
    h_last = h[:, -1]
    return out.astype(u.dtype), jnp.real(h_last).astype(h0_re.dtype), jnp.imag(h_last).astype(h0_im.dtype)


def _rec_mixer(h, w_in, conv_w, lam_re, lam_im, log_dt, b_re, b_im, c_re, c_im, d_skip, w_glu, w_out,
               conv_buf, h0_re, h0_im):
    proj = h @ w_in
    v, c_gate, b_gate, u = jnp.split(proj, [D_CONV, 2 * D_CONV, 3 * D_CONV], axis=-1)
    y_conv, new_buf = _short_conv(v, c_gate, b_gate, conv_w, conv_buf)
    y_ssm, h_re, h_im = _s5(u, lam_re, lam_im, log_dt, b_re, b_im, c_re, c_im, d_skip, w_glu, h0_re, h0_im)
    out = jnp.concatenate([y_conv, y_ssm], axis=-1) @ w_out
    return out, new_buf, h_re, h_im


def _diff_lambda(lq1, lk1, lq2, lk2, lambda_init):
    f32 = jnp.float32
    return (jnp.exp(jnp.sum(lq1.astype(f32) * lk1.astype(f32)))
            - jnp.exp(jnp.sum(lq2.astype(f32) * lk2.astype(f32))) + lambda_init)


def _diff_project(h, w_qkv):
    b, t, _ = h.shape
    q, k, v = jnp.split(h @ w_qkv, 3, axis=-1)
    q = q.reshape(b, t, N_DIFF_HEADS, 2, DIFF_HEAD_DIM)
    k = k.reshape(b, t, N_DIFF_HEADS, 2, DIFF_HEAD_DIM)
    v = v.reshape(b, t, N_DIFF_HEADS, DIFF_V_DIM)
    return q, k, v


def _diff_out(o, subln, lambda_init, w_o):
    b, t = o.shape[:2]
    o = _rmsnorm(o, subln) * (1.0 - lambda_init)
    return o.reshape(b, t, D_ATTN) @ w_o


def _diff_attn_prompt(q, k, v, lam):
    f32 = jnp.float32
    b, t = q.shape[:2]
    nb = t // Q_BLOCK
    scale = DIFF_HEAD_DIM ** -0.5
    qb = jnp.moveaxis(q.reshape(b, nb, Q_BLOCK, N_DIFF_HEADS, 2, DIFF_HEAD_DIM), 1, 0)
    kf = k.astype(f32)
    vf = v.astype(f32)
    k_pos = jnp.arange(t)

    def block(args):
        qi, i = args
        s = jnp.einsum('bqhcd,bkhcd->bhcqk', qi.astype(f32), kf) * scale
        q_pos = i * Q_BLOCK + jnp.arange(Q_BLOCK)
        s = jnp.where(k_pos[None, :] <= q_pos[:, None], s, -jnp.inf)
        p = jax.nn.softmax(s, axis=-1)
        w = p[:, :, 0] - lam * p[:, :, 1]
        return jnp.einsum('bhqk,bkhe->bqhe', w, vf).astype(v.dtype)

    o = lax.map(block, (qb, jnp.arange(nb)))
    return jnp.moveaxis(o, 0, 1).reshape(b, t, N_DIFF_HEADS, DIFF_V_DIM)


def _online_update(carry, s, v):
    m, l, acc = carry
    m_new = jnp.maximum(m, jnp.max(s, axis=-1))
    corr = jnp.exp(m - m_new)
    p = jnp.exp(s - m_new[..., None])
    l = l * corr + jnp.sum(p, axis=-1)
    acc = acc * corr[..., None] + jnp.einsum('bhcqk,bkhe->bhcqe', p, v.astype(jnp.float32))
    return m_new, l, acc


def _diff_attn_sample(q, k_new, v_new, cache_k, cache_v, layer, page_table, lam):
    f32 = jnp.float32
    bd, sq = q.shape[:2]
    qs = q.astype(f32) * DIFF_HEAD_DIM ** -0.5
    init = (jnp.full((bd, N_DIFF_HEADS, 2, sq), -jnp.inf, f32),
            jnp.zeros((bd, N_DIFF_HEADS, 2, sq), f32),
            jnp.zeros((bd, N_DIFF_HEADS, 2, sq, DIFF_V_DIM), f32))

    def page_step(carry, page_ids):
        kp = cache_k[layer, page_ids]
        vp = cache_v[layer, page_ids]
        s = jnp.einsum('bqhcd,bkhcd->bhcqk', qs, kp.astype(f32))
        return _online_update(carry, s, vp), None

    carry, _ = lax.scan(page_step, init, page_table.T)
    s_new = jnp.einsum('bqhcd,bkhcd->bhcqk', qs, k_new.astype(f32))
    causal = jnp.arange(sq)[None, :] <= jnp.arange(sq)[:, None]
    s_new = jnp.where(causal, s_new, -jnp.inf)
    _, l, acc = _online_update(carry, s_new, v_new)
    o = acc / l[..., None]
    w = o[:, :, 0] - lam * o[:, :, 1]
    return jnp.moveaxis(w, 1, 2).astype(v_new.dtype)


def _mem_kv(mem, w_k, w_v):
    b = mem.shape[0]
    k = (mem @ w_k).reshape(b, N_MEM, N_CROSS_HEADS, CROSS_HEAD_DIM)
    v = (mem @ w_v).reshape(b, N_MEM, N_CROSS_HEADS, CROSS_HEAD_DIM)
    return k, v


def _cross_attn(h, mem_k, mem_v, w_q, w_o):
    b, t, _ = h.shape
    q = (h @ w_q).reshape(b, t, N_CROSS_HEADS, CROSS_HEAD_DIM)
    s = jnp.einsum('bqhd,bkhd->bhqk', q.astype(jnp.float32), mem_k.astype(jnp.float32)) * CROSS_HEAD_DIM ** -0.5
    p = jax.nn.softmax(s, axis=-1)
    o = jnp.einsum('bhqk,bkhd->bqhd', p, mem_v.astype(jnp.float32)).astype(h.dtype)
    return o.reshape(b, t, D_CROSS) @ w_o


def _hier_moe(h, w_group, b_group, w_router, b_router, w_gate, w_up, w_down):
    f32 = jnp.float32
    hf = h.reshape(-1, D_MODEL)
    t = hf.shape[0]
    rows = jnp.arange(t)
    g_logits = (hf @ w_group + b_group).astype(f32)
    g_idx = jnp.argmax(g_logits, axis=-1)
    g_w = jax.nn.softmax(g_logits, axis=-1)[rows, g_idx]
    e_logits = (hf @ w_router + b_router).astype(f32).reshape(t, N_GROUPS, EXPERTS_PER_GROUP)
    e_prob = jax.nn.softmax(e_logits[rows, g_idx], axis=-1)
    top_p, top_i = lax.top_k(e_prob, TOP_K)
    gates = g_w[:, None] * top_p / jnp.sum(top_p, axis=-1, keepdims=True)
    expert_id = g_idx[:, None] * EXPERTS_PER_GROUP + top_i
    tk = t * TOP_K
    e_flat = expert_id.reshape(-1)
    g_flat = gates.reshape(-1)
    tok = jnp.arange(tk) // TOP_K
    order = jnp.argsort(e_flat)
    e_sorted = e_flat[order]
    counts = jnp.zeros((N_EXPERTS,), jnp.int32).at[e_flat].add(1)
    starts = jnp.cumsum(counts) - counts
    padded = ((counts + MOE_BLOCK - 1) // MOE_BLOCK) * MOE_BLOCK
    pstarts = jnp.cumsum(padded) - padded
    pends = pstarts + padded
    dest = pstarts[e_sorted] + (jnp.arange(tk) - starts[e_sorted])
    n_slots = ((tk + MOE_BLOCK - 1) // MOE_BLOCK + N_EXPERTS) * MOE_BLOCK
    n_blocks = n_slots // MOE_BLOCK
    slot_tok = jnp.full((n_slots,), t, jnp.int32).at[dest].set(tok[order].astype(jnp.int32))
    slot_gate = jnp.zeros((n_slots,), f32).at[dest].set(g_flat[order])
    block_start = jnp.arange(n_blocks) * MOE_BLOCK
    block_e = jnp.minimum(jnp.sum(block_start[:, None] >= pends[None, :], axis=1), N_EXPERTS - 1)
    x_pad = jnp.concatenate([hf, jnp.zeros((1, D_MODEL), hf.dtype)], axis=0)
    xs = x_pad[slot_tok].reshape(n_blocks, MOE_BLOCK, D_MODEL)

    def expert_block(args):
        xb, e = args
        return (jax.nn.silu(xb @ w_gate[e]) * (xb @ w_up[e])) @ w_down[e]

    ys = lax.map(expert_block, (xs, block_e)).reshape(n_slots, D_MODEL)
    out = jnp.zeros((t + 1, D_MODEL), f32).at[slot_tok].add(ys.astype(f32) * slot_gate[:, None])[:t]
    return out.astype(h.dtype).reshape(h.shape)


def setup_inputs(seed: int = 0) -> dict:
    key = jax.random.key(seed)
    keys = iter(jax.random.split(key, 64))
    f32 = jnp.float32

    def nrm(shape, scale=1.0):
        return scale * jax.random.normal(next(keys), shape, f32)

    R, A = N_REC_LAYERS, N_ATTN_LAYERS
    n_pages = PAST_LEN // PAGE_SIZE
    n_used = DEC_BATCH * n_pages
    n_pool = n_used + n_used // 4
    x_prompt = nrm((BATCH, SEQ, D_MODEL))
    x_sample = nrm((DEC_BATCH, DEC_SEQ, D_MODEL))
    state_conv = nrm((R, DEC_BATCH, CONV_WIDTH - 1, D_CONV))
    state_s5_re = nrm((R, DEC_BATCH, N_SSM_GROUPS, SSM_STATE), 0.1)
    state_s5_im = nrm((R, DEC_BATCH, N_SSM_GROUPS, SSM_STATE), 0.1)
    cache_k = nrm((A, n_pool, PAGE_SIZE, N_DIFF_HEADS, 2, DIFF_HEAD_DIM))
    cache_v = nrm((A, n_pool, PAGE_SIZE, N_DIFF_HEADS, DIFF_V_DIM))
    cache_mem_k = nrm((DEPTH, DEC_BATCH, N_MEM, N_CROSS_HEADS, CROSS_HEAD_DIM))
    cache_mem_v = nrm((DEPTH, DEC_BATCH, N_MEM, N_CROSS_HEADS, CROSS_HEAD_DIM))
    page_table = jax.random.permutation(next(keys), n_pool)[:n_used].reshape(DEC_BATCH, n_pages).astype(jnp.int32)
    mem_prompt = nrm((BATCH, N_MEM, D_MODEL))
    return {
        'x_prompt': x_prompt,
        'x_sample': x_sample,
        'state_conv': state_conv,
        'state_s5_re': state_s5_re,
        'state_s5_im': state_s5_im,
        'cache_k': cache_k,
        'cache_v': cache_v,
        'cache_mem_k': cache_mem_k,
        'cache_mem_v': cache_mem_v,
        'page_table': page_table,
        'mem_prompt': mem_prompt,
        'norm_mix': 1.0 + nrm((DEPTH, D_MODEL), 0.01),
        'norm_cross': 1.0 + nrm((DEPTH, D_MODEL), 0.01),
        'norm_ffn': 1.0 + nrm((DEPTH, D_MODEL), 0.01),
        'norm_final': 1.0 + nrm((D_MODEL,), 0.01),
        'w_in_rec': nrm((R, D_MODEL, 3 * D_CONV + D_SSM), D_MODEL ** -0.5),
        'conv_w': nrm((R, CONV_WIDTH, D_CONV), CONV_WIDTH ** -0.5),
        'ssm_lambda_re': -0.5 + nrm((R, N_SSM_GROUPS, SSM_STATE), 0.01),
        'ssm_lambda_im': jnp.pi * jnp.arange(SSM_STATE, dtype=f32) + nrm((R, N_SSM_GROUPS, SSM_STATE), 0.01),
        'ssm_log_dt': jax.random.uniform(next(keys), (R, N_SSM_GROUPS), f32, math.log(1e-3), math.log(1e-1)),
        'ssm_b_re': nrm((R, N_SSM_GROUPS, SSM_STATE, SSM_GROUP), (2 * SSM_GROUP) ** -0.5),
        'ssm_b_im': nrm((R, N_SSM_GROUPS, SSM_STATE, SSM_GROUP), (2 * SSM_GROUP) ** -0.5),
        'ssm_c_re': nrm((R, N_SSM_GROUPS, SSM_GROUP, SSM_STATE), SSM_STATE ** -0.5),
        'ssm_c_im': nrm((R, N_SSM_GROUPS, SSM_GROUP, SSM_STATE), SSM_STATE ** -0.5),
        'ssm_d': nrm((R, D_SSM)),
        'ssm_w_glu': nrm((R, D_SSM, D_SSM), D_SSM ** -0.5),
        'w_out_rec': nrm((R, D_CONV + D_SSM, D_MODEL), (D_CONV + D_SSM) ** -0.5),
        'w_qkv': nrm((A, D_MODEL, 3 * D_ATTN), D_MODEL ** -0.5),
        'diff_lambda_q1': nrm((A, DIFF_HEAD_DIM), 0.1),
        'diff_lambda_k1': nrm((A, DIFF_HEAD_DIM), 0.1),
        'diff_lambda_q2': nrm((A, DIFF_HEAD_DIM), 0.1),
        'diff_lambda_k2': nrm((A, DIFF_HEAD_DIM), 0.1),
        'diff_subln': 1.0 + nrm((A, DIFF_V_DIM), 0.01),
        'w_o_attn': nrm((A, D_ATTN, D_MODEL), D_ATTN ** -0.5),
        'w_cross_q': nrm((DEPTH, D_MODEL, D_CROSS), D_MODEL ** -0.5),
        'w_cross_k': nrm((DEPTH, D_MODEL, D_CROSS), D_MODEL ** -0.5),
        'w_cross_v': nrm((DEPTH, D_MODEL, D_CROSS), D_MODEL ** -0.5),
        'w_cross_o': nrm((DEPTH, D_CROSS, D_MODEL), D_CROSS ** -0.5),
        'w_group': nrm((DEPTH, D_MODEL, N_GROUPS), D_MODEL ** -0.5),
        'b_group': nrm((DEPTH, N_GROUPS), 0.01),
        'w_router': nrm((DEPTH, D_MODEL, N_EXPERTS), D_MODEL ** -0.5),
        'b_router': nrm((DEPTH, N_EXPERTS), 0.01),
        'w_exp_gate': nrm((DEPTH, N_EXPERTS, D_MODEL, D_EXPERT), D_MODEL ** -0.5),
        'w_exp_up': nrm((DEPTH, N_EXPERTS, D_MODEL, D_EXPERT), D_MODEL ** -0.5),
        'w_exp_down': nrm((DEPTH, N_EXPERTS, D_EXPERT, D_MODEL), D_EXPERT ** -0.5),
    }


def reference(x_prompt, x_sample, state_conv, state_s5_re, state_s5_im, cache_k, cache_v, cache_mem_k,
              cache_mem_v, page_table, mem_prompt, norm_mix, norm_cross, norm_ffn, norm_final, w_in_rec, conv_w,
              ssm_lambda_re, ssm_lambda_im, ssm_log_dt, ssm_b_re, ssm_b_im, ssm_c_re, ssm_c_im, ssm_d, ssm_w_glu,
              w_out_rec, w_qkv, diff_lambda_q1, diff_lambda_k1, diff_lambda_q2, diff_lambda_k2, diff_subln,
              w_o_attn, w_cross_q, w_cross_k, w_cross_v, w_cross_o, w_group, b_group, w_router, b_router,
              w_exp_gate, w_exp_up, w_exp_down):
    xp, xs = x_prompt, x_sample
    bp = xp.shape[0]
    conv_p, conv_s, s5re_p, s5im_p, s5re_s, s5im_s = [], [], [], [], [], []
    k_p, v_p, k_s, v_s, memk_p, memv_p = [], [], [], [], [], []
    for layer in range(DEPTH):
        hp = _rmsnorm(xp, norm_mix[layer])
        hs = _rmsnorm(xs, norm_mix[layer])
        if layer % 2 == 0:
            r = layer // 2
            rec = (w_in_rec[r], conv_w[r], ssm_lambda_re[r], ssm_lambda_im[r], ssm_log_dt[r], ssm_b_re[r],
                   ssm_b_im[r], ssm_c_re[r], ssm_c_im[r], ssm_d[r], ssm_w_glu[r], w_out_rec[r])
            zero_buf = jnp.zeros((bp, CONV_WIDTH - 1, D_CONV), xp.dtype)
            zero_h = jnp.zeros((bp, N_SSM_GROUPS, SSM_STATE), jnp.float32)
            yp, cbp, hrp, hip = _rec_mixer(hp, *rec, zero_buf, zero_h, zero_h)
            ys, cbs, hrs, his = _rec_mixer(hs, *rec, state_conv[r], state_s5_re[r], state_s5_im[r])
            conv_p.append(cbp)
            conv_s.append(cbs)
            s5re_p.append(hrp)
            s5im_p.append(hip)
            s5re_s.append(hrs)
            s5im_s.append(his)
        else:
            a = layer // 2
            lambda_init = 0.8 - 0.6 * math.exp(-0.3 * layer)
            lam = _diff_lambda(diff_lambda_q1[a], diff_lambda_k1[a], diff_lambda_q2[a], diff_lambda_k2[a], lambda_init)
            qp, kp_, vp_ = _diff_project(hp, w_qkv[a])
            yp = _diff_out(_diff_attn_prompt(qp, kp_, vp_, lam), diff_subln[a], lambda_init, w_o_attn[a])
            qs, ks_, vs_ = _diff_project(hs, w_qkv[a])
            os_ = _diff_attn_sample(qs, ks_, vs_, cache_k, cache_v, a, page_table, lam)
            ys = _diff_out(os_, diff_subln[a], lambda_init, w_o_attn[a])
            k_p.append(kp_)
            v_p.append(vp_)
            k_s.append(ks_)
            v_s.append(vs_)
        xp = xp + yp
        xs = xs + ys
        mk, mv = _mem_kv(mem_prompt, w_cross_k[layer], w_cross_v[layer])
        memk_p.append(mk)
        memv_p.append(mv)
        xp = xp + _cross_attn(_rmsnorm(xp, norm_cross[layer]), mk, mv, w_cross_q[layer], w_cross_o[layer])
        xs = xs + _cross_attn(_rmsnorm(xs, norm_cross[layer]), cache_mem_k[layer], cache_mem_v[layer],
                              w_cross_q[layer], w_cross_o[layer])
        moe = (w_group[layer], b_group[layer], w_router[layer], b_router[layer],
               w_exp_gate[layer], w_exp_up[layer], w_exp_down[layer])
        xp = xp + _hier_moe(_rmsnorm(xp, norm_ffn[layer]), *moe)
        xs = xs + _hier_moe(_rmsnorm(xs, norm_ffn[layer]), *moe)
    y_prompt = _rmsnorm(xp, norm_final)
    y_sample = _rmsnorm(xs, norm_final)
    new_conv_prompt = jnp.stack(conv_p)
    new_conv_sample = jnp.stack(conv_s)
    new_s5_re_prompt = jnp.stack(s5re_p)
    new_s5_im_prompt = jnp.stack(s5im_p)
    new_s5_re_sample = jnp.stack(s5re_s)
    new_s5_im_sample = jnp.stack(s5im_s)
    new_k_prompt = jnp.stack(k_p)
    new_v_prompt = jnp.stack(v_p)
    new_k_sample = jnp.stack(k_s)
    new_v_sample = jnp.stack(v_s)
    new_mem_k_prompt = jnp.stack(memk_p)
    new_mem_v_prompt = jnp.stack(memv_p)
    return (y_prompt, y_sample, new_conv_prompt, new_conv_sample, new_s5_re_prompt, new_s5_im_prompt,
            new_s5_re_sample, new_s5_im_sample, new_k_prompt, new_v_prompt, new_k_sample, new_v_sample,
            new_mem_k_prompt, new_mem_v_prompt)
```

```python
import functools
import math

import jax
import jax.numpy as jnp
from jax import lax
from jax.experimental import pallas as pl
from jax.experimental.pallas import tpu as pltpu

F32 = jnp.float32
BF16 = jnp.bfloat16
I32 = jnp.int32
EPS = 1e-6

LANES = 128
SUBLANES = 8
VMEM_LIMIT_BYTES = 56 * 1024 * 1024

SSM_GROUP = 16
SSM_STATE = 64
S5_CHUNK = 16
DIFF_HEAD_DIM = 64
DIFF_V_DIM = 128
CROSS_HEAD_DIM = 128
N_GROUPS = 4
EXPERTS_PER_GROUP = 8
N_EXPERTS = N_GROUPS * EXPERTS_PER_GROUP
MOE_ROWS = 512
MOE_TOK_TILE = 128
PAGES_PER_STEP = 4


def _params(semantics):
    return pltpu.CompilerParams(dimension_semantics=semantics, vmem_limit_bytes=VMEM_LIMIT_BYTES)


def _rms(v, g):
    return v * lax.rsqrt(jnp.mean(v * v, axis=-1, keepdims=True) + EPS) * g


def _nt_dot(a, b):
    return lax.dot_general(a, b, (((1,), (1,)), ((), ())), preferred_element_type=F32)


def _mm_body(*refs, n_x, prologue, epilogue, n_out, direct):
    x_refs = refs[:n_x]
    w_refs = refs[n_x:2 * n_x]
    pos = 2 * n_x
    g_ref = e_ref = None
    if prologue == "norm":
        g_ref = refs[pos]
        pos += 1
    if epilogue is not None:
        e_ref = refs[pos]
        pos += 1
    o_refs = refs[pos:pos + n_out]
    xb_refs = refs[pos + n_out:]

    if not direct:
        @pl.when(pl.program_id(1) == 0)
        def _():
            for x_ref, xb_ref in zip(x_refs, xb_refs):
                v = x_ref[...].astype(F32)
                if prologue == "norm":
                    v = _rms(v, g_ref[...])
                elif prologue == "gelu":
                    v = jax.nn.gelu(v)
                xb_ref[...] = v.astype(BF16)

    acc = None
    for k in range(n_x):
        lhs = x_refs[k][...] if direct else xb_refs[k][...]
        d = jnp.dot(lhs, w_refs[k][...].astype(BF16), preferred_element_type=F32)
        acc = d if acc is None else acc + d
    if epilogue == "res":
        acc = acc + e_ref[...]
    elif epilogue == "glu":
        s = jax.nn.gelu(e_ref[...])
        acc = s * jax.nn.sigmoid(acc)
    for o_ref in o_refs:
        o_ref[...] = acc.astype(o_ref.dtype)


def _mm(xs, ws, *, m, n, layer=0, w_rows=None, col_off=0, x_off=0, prologue=None, g=None,
        epilogue=None, e=None, e_off=0, out_dtypes=(F32,), name="mm"):
    n_x = len(xs)
    w_rows = w_rows or (0,) * n_x
    tm = min(m, 1024)
    tn = min(n, 512 if m > 128 else 1024)
    assert m % tm == 0 and n % tn == 0 and col_off % tn == 0 and x_off % tm == 0 and e_off % tm == 0
    xo, eo, co = x_off // tm, e_off // tm, col_off // tn
    direct = prologue is None and all(x.dtype == BF16 for x in xs)

    in_specs, args = [], []
    for x in xs:
        in_specs.append(pl.BlockSpec((tm, x.shape[1]), lambda i, j: (i + xo, 0)))
        args.append(x)
    for x, w, rb in zip(xs, ws, w_rows):
        in_specs.append(pl.BlockSpec((None, x.shape[1], tn), lambda i, j, rb=rb: (layer, rb, j + co)))
        args.append(w)
    if prologue == "norm":
        in_specs.append(pl.BlockSpec((1, g.shape[1]), lambda i, j: (0, 0)))
        args.append(g)
    if epilogue is not None:
        in_specs.append(pl.BlockSpec((tm, tn), lambda i, j: (i + eo, j)))
        args.append(e)
    out_specs = [pl.BlockSpec((tm, tn), lambda i, j: (i, j)) for _ in out_dtypes]
    out_shape = [jax.ShapeDtypeStruct((m, n), dt) for dt in out_dtypes]
    scratch = [] if direct else [pltpu.VMEM((tm, x.shape[1]), BF16) for x in xs]
    outs = pl.pallas_call(
        functools.partial(_mm_body, n_x=n_x, prologue=prologue, epilogue=epilogue,
                          n_out=len(out_dtypes), direct=direct),
        grid=(m // tm, n // tn),
        in_specs=in_specs, out_specs=out_specs, out_shape=out_shape,
        scratch_shapes=scratch,
        compiler_params=_params(("parallel", "arbitrary")),
        name=name,
    )(*args)
    return outs[0] if len(out_dtypes) == 1 else outs


def _rec_in_body(x_ref, g_ref, wv_ref, wc_ref, wb_ref, wu_ref, cw_ref, *rest, sample, tiles_per_seq):
    if sample:
        b0_ref, b1_ref, yc_ref, u_ref, uc_ref, xb_ref = rest
    else:
        yc_ref, u_ref, tail_ref, xb_ref, carry_ref = rest
    i = pl.program_id(0)
    j = pl.program_id(1)

    @pl.when(j == 0)
    def _():
        xb_ref[...] = _rms(x_ref[...], g_ref[...]).astype(BF16)

    xb = xb_ref[...]

    def proj(w_ref):
        return jnp.dot(xb, w_ref[...].astype(BF16), preferred_element_type=F32)

    v = proj(wv_ref)
    c = proj(wc_ref)
    b = proj(wb_ref)
    u_ref[...] = proj(wu_ref)
    uc = c * v
    cw = cw_ref[...]
    if sample:
        y = cw[0:1] * b0_ref[...] + cw[1:2] * b1_ref[...] + cw[2:3] * uc
        uc_ref[...] = uc
    else:
        tm = uc.shape[0]

        @pl.when(i % tiles_per_seq == 0)
        def _():
            carry_ref[j, 0:2, :] = jnp.zeros((2, uc.shape[1]), F32)

        prev = carry_ref[j, 0:2, :]
        row = lax.broadcasted_iota(I32, uc.shape, 0)
        s1 = jnp.where(row == 0, prev[1:2], pltpu.roll(uc, 1, 0))
        s2 = jnp.where(row == 0, prev[0:1], jnp.where(row == 1, prev[1:2], pltpu.roll(uc, 2, 0)))
        y = cw[0:1] * s2 + cw[1:2] * s1 + cw[2:3] * uc
        tail = uc[tm - 2:tm, :]
        tail_ref[...] = tail
        carry_ref[j, 0:2, :] = tail
    yc_ref[...] = (b * y).astype(yc_ref.dtype)


def _rec_in(x, x_off, m, seq, g, w_in, layer, conv_w, buf=None):
    d = x.shape[1]
    c = conv_w.shape[1]
    sample = buf is not None
    tm = min(m, 1024)
    tn = 256
    nj = c // tn
    xo = x_off // tm
    assert m % tm == 0 and x_off % tm == 0 and (sample or seq % tm == 0)
    in_specs = [pl.BlockSpec((tm, d), lambda i, j: (i + xo, 0)),
                pl.BlockSpec((1, d), lambda i, j: (0, 0))]
    for k in range(4):
        in_specs.append(pl.BlockSpec((None, d, tn), lambda i, j, k=k: (layer, 0, k * nj + j)))
    in_specs.append(pl.BlockSpec((3, tn), lambda i, j: (0, j)))
    args = [x, g, w_in, w_in, w_in, w_in, conv_w]
    blk = pl.BlockSpec((tm, tn), lambda i, j: (i, j))
    out_specs = [blk, blk]
    out_shape = [jax.ShapeDtypeStruct((m, c), BF16), jax.ShapeDtypeStruct((m, c), F32)]
    scratch = [pltpu.VMEM((tm, d), BF16)]
    if sample:
        in_specs += [blk, blk]
        args += [buf[:, 0], buf[:, 1]]
        out_specs.append(blk)
        out_shape.append(jax.ShapeDtypeStruct((m, c), F32))
    else:
        out_specs.append(pl.BlockSpec((None, 2, tn), lambda i, j: (i, 0, j)))
        out_shape.append(jax.ShapeDtypeStruct((m // tm, 2, c), F32))
        scratch.append(pltpu.VMEM((nj, SUBLANES, tn), F32))
    return pl.pallas_call(
        functools.partial(_rec_in_body, sample=sample, tiles_per_seq=max(seq // tm, 1)),
        grid=(m // tm, nj),
        in_specs=in_specs, out_specs=out_specs, out_shape=out_shape,
        scratch_shapes=scratch,
        compiler_params=_params(("arbitrary", "arbitrary")),
        name="rec_in_sample" if sample else "rec_in_prompt",
    )(*args)


def _s5_tables(lam_re, lam_im, log_dt, b_re, b_im, c_re, c_im, d_skip, chunk, gs):
    hi = lax.Precision.HIGHEST
    n_g, n_s = lam_re.shape
    dt = jnp.exp(log_dt.astype(F32))[:, None]
    ar, ai = lam_re.astype(F32) * dt, lam_im.astype(F32) * dt
    k = jnp.arange(chunk + 1, dtype=F32)[:, None, None]
    mag = jnp.exp(k * ar)
    pw_re, pw_im = mag * jnp.cos(k * ai), mag * jnp.sin(k * ai)
    xr, xi = pw_re[1] - 1.0, pw_im[1]
    den = lam_re * lam_re + lam_im * lam_im
    qr, qi = (xr * lam_re + xi * lam_im) / den, (xi * lam_re - xr * lam_im) / den
    bb_re = qr[..., None] * b_re - qi[..., None] * b_im
    bb_im = qr[..., None] * b_im + qi[..., None] * b_re
    cp_re = c_re[None] * pw_re[:, :, None, :] - c_im[None] * pw_im[:, :, None, :]
    cp_im = c_re[None] * pw_im[:, :, None, :] + c_im[None] * pw_re[:, :, None, :]
    kern = (jnp.einsum("tgcn,gnd->tgcd", cp_re[:chunk], bb_re, precision=hi)
            - jnp.einsum("tgcn,gnd->tgcd", cp_im[:chunk], bb_im, precision=hi))
    pos = jnp.arange(chunk)
    lag = pos[None, :] - pos[:, None]
    toe = jnp.where((lag >= 0)[:, :, None, None, None], kern[jnp.clip(lag, 0)], 0.0)
    w1 = chunk * SSM_GROUP
    a_t = jnp.transpose(toe, (2, 0, 4, 1, 3)).reshape(n_g, w1, w1)
    e_t = jnp.concatenate([jnp.transpose(cp_re[1:], (1, 3, 0, 2)),
                           -jnp.transpose(cp_im[1:], (1, 3, 0, 2))], axis=1).reshape(n_g, 2 * n_s, w1)
    rev_re, rev_im = pw_re[chunk - 1::-1][:chunk], pw_im[chunk - 1::-1][:chunk]
    pb_re = rev_re[..., None] * bb_re[None] - rev_im[..., None] * bb_im[None]
    pb_im = rev_re[..., None] * bb_im[None] + rev_im[..., None] * bb_re[None]
    s_t = jnp.concatenate([jnp.transpose(pb_re, (1, 0, 3, 2)),
                           jnp.transpose(pb_im, (1, 0, 3, 2))], axis=-1).reshape(n_g, w1, 2 * n_s)

    def bdiag(t):
        n_sg = n_g // gs
        t = t.reshape(n_sg, gs, t.shape[1], t.shape[2])
        eye = jnp.eye(gs, dtype=t.dtype)
        t = t[:, :, :, None, :] * eye[None, :, None, :, None]
        return t.reshape(n_sg, gs * t.shape[2], gs * t.shape[4]).astype(BF16)

    lam_a = jnp.concatenate([pw_re[chunk], pw_re[chunk]], axis=-1)
    lam_b = jnp.concatenate([-pw_im[chunk], pw_im[chunk]], axis=-1)
    d_t = jnp.broadcast_to(d_skip.astype(F32).reshape(n_g, 1, SSM_GROUP), (n_g, chunk, SSM_GROUP))
    return bdiag(s_t), bdiag(a_t), bdiag(e_t), lam_a, lam_b, d_t.reshape(1, n_g * w1)


def _s5_local_body(u_ref, st_ref, o_ref):
    o_ref[...] = jnp.dot(u_ref[...].astype(BF16), st_ref[...], preferred_element_type=F32)


def _s5_scan_body(s_ref, h0_ref, la_ref, lb_ref, hc_ref, hl_ref, *, n_chunks):
    la = la_ref[...]
    lb = lb_ref[...]

    def step(c, h):
        hc_ref[c] = h
        return la * h + lb * pltpu.roll(h, SSM_STATE, 1) + s_ref[c]

    hl_ref[...] = lax.fori_loop(0, n_chunks, step, h0_ref[...])


def _s5_out_body(u_ref, at_ref, hc_ref, et_ref, d_ref, y_ref):
    u = u_ref[...]
    y = jnp.dot(u.astype(BF16), at_ref[...], preferred_element_type=F32)
    y = y + jnp.dot(hc_ref[...].astype(BF16), et_ref[...], preferred_element_type=F32)
    y_ref[...] = y + u * d_ref[...]


def _s5(u, bsz, seq, tables, h0, chunk, gs):
    s_t, a_t, e_t, lam_a, lam_b, d_t = tables
    n_g = u.shape[1] // SSM_GROUP
    n_sg = n_g // gs
    n_c = seq // chunk
    rows = n_c * bsz
    w = gs * chunk * SSM_GROUP
    sw = gs * 2 * SSM_STATE
    if n_c == 1 and chunk == 1:
        up = u
    else:
        up = jnp.transpose(u.reshape(bsz, n_c, chunk, n_g, SSM_GROUP), (1, 0, 3, 2, 4)).reshape(rows, n_g * chunk * SSM_GROUP)

    s_loc = pl.pallas_call(
        _s5_local_body,
        grid=(n_sg,),
        in_specs=[pl.BlockSpec((rows, w), lambda s: (0, s)),
                  pl.BlockSpec((None, w, sw), lambda s: (s, 0, 0))],
        out_specs=pl.BlockSpec((rows, sw), lambda s: (0, s)),
        out_shape=jax.ShapeDtypeStruct((rows, n_g * 2 * SSM_STATE), F32),
        compiler_params=_params(("parallel",)),
        name="s5_local",
    )(up, s_t)

    r2 = bsz * n_g
    rb = min(r2, 1024 if n_c == 1 else 64)
    lane = 2 * SSM_STATE
    la = jnp.broadcast_to(lam_a[None], (bsz, n_g, lane)).reshape(r2, lane)
    lb = jnp.broadcast_to(lam_b[None], (bsz, n_g, lane)).reshape(r2, lane)
    hc, h_last = pl.pallas_call(
        functools.partial(_s5_scan_body, n_chunks=n_c),
        grid=(r2 // rb,),
        in_specs=[pl.BlockSpec((n_c, rb, lane), lambda r: (0, r, 0)),
                  pl.BlockSpec((rb, lane), lambda r: (r, 0)),
                  pl.BlockSpec((rb, lane), lambda r: (r, 0)),
                  pl.BlockSpec((rb, lane), lambda r: (r, 0))],
        out_specs=[pl.BlockSpec((n_c, rb, lane), lambda r: (0, r, 0)),
                   pl.BlockSpec((rb, lane), lambda r: (r, 0))],
        out_shape=[jax.ShapeDtypeStruct((n_c, r2, lane), F32),
                   jax.ShapeDtypeStruct((r2, lane), F32)],
        compiler_params=_params(("parallel",)),
        name="s5_scan",
    )(s_loc.reshape(n_c, r2, lane), h0.reshape(r2, lane), la, lb)

    yp = pl.pallas_call(
        _s5_out_body,
        grid=(n_sg,),
        in_specs=[pl.BlockSpec((rows, w), lambda s: (0, s)),
                  pl.BlockSpec((None, w, w), lambda s: (s, 0, 0)),
                  pl.BlockSpec((rows, sw), lambda s: (0, s)),
                  pl.BlockSpec((None, sw, w), lambda s: (s, 0, 0)),
                  pl.BlockSpec((1, w), lambda s: (0, s))],
        out_specs=pl.BlockSpec((rows, w), lambda s: (0, s)),
        out_shape=jax.ShapeDtypeStruct((rows, n_g * chunk * SSM_GROUP), F32),
        compiler_params=_params(("parallel",)),
        name="s5_out",
    )(up, a_t, hc.reshape(rows, n_g * lane), e_t, d_t)
    if n_c == 1 and chunk == 1:
        y = yp
    else:
        y = jnp.transpose(yp.reshape(n_c, bsz, n_g, chunk, SSM_GROUP), (1, 0, 3, 2, 4)).reshape(bsz * seq, n_g * SSM_GROUP)
    return y, h_last.reshape(bsz, n_g, lane)


def _cross_prompt_body(q_ref, k_ref, v_ref, o_ref, *, n_heads):
    scale = CROSS_HEAD_DIM ** -0.5
    for h in range(n_heads):
        sl = slice(h * CROSS_HEAD_DIM, (h + 1) * CROSS_HEAD_DIM)
        s = _nt_dot(q_ref[:, sl], k_ref[:, sl].astype(BF16)) * scale
        e = jnp.exp(s - jnp.max(s, axis=-1, keepdims=True))
        l = jnp.sum(e, axis=-1, keepdims=True)
        o = jnp.dot(e.astype(BF16), v_ref[:, sl].astype(BF16), preferred_element_type=F32)
        o_ref[:, sl] = (o / l).astype(o_ref.dtype)


def _cross_prompt(q, mem_k, mem_v, bsz, seq):
    dc = q.shape[1]
    n_mem = mem_k.shape[0] // bsz
    tq = min(seq, 1024)
    nq = seq // tq
    return pl.pallas_call(
        functools.partial(_cross_prompt_body, n_heads=dc // CROSS_HEAD_DIM),
        grid=(bsz, nq),
        in_specs=[pl.BlockSpec((tq, dc), lambda b, i: (b * nq + i, 0)),
                  pl.BlockSpec((n_mem, dc), lambda b, i: (b, 0)),
                  pl.BlockSpec((n_mem, dc), lambda b, i: (b, 0))],
        out_specs=pl.BlockSpec((tq, dc), lambda b, i: (b * nq + i, 0)),
        out_shape=jax.ShapeDtypeStruct((bsz * seq, dc), BF16),
        compiler_params=_params(("parallel", "parallel")),
        name="cross_prompt",
    )(q, mem_k, mem_v)


def _cross_sample_body(q_ref, k_ref, v_ref, seg_ref, segt_ref, o_ref):
    bs, n_mem, dc = k_ref.shape
    scale = CROSS_HEAD_DIM ** -0.5
    q = q_ref[...]
    kq = (k_ref[...] * q[:, None, :]).reshape(bs * n_mem, dc).astype(BF16)
    s = (jnp.dot(kq, seg_ref[...], preferred_element_type=F32) * scale).reshape(bs, n_mem, LANES)
    e = jnp.exp(s - jnp.max(s, axis=1, keepdims=True))
    p = (e / jnp.sum(e, axis=1, keepdims=True)).reshape(bs * n_mem, LANES).astype(BF16)
    pe = jnp.dot(p, segt_ref[...], preferred_element_type=F32).reshape(bs, n_mem, dc)
    o_ref[...] = jnp.sum(pe * v_ref[...], axis=1).astype(o_ref.dtype)


def _cross_sample(q, cache_k, cache_v, layer):
    bsz, dc = q.shape
    n_mem = cache_k.shape[2]
    bs = SUBLANES
    head = lax.broadcasted_iota(I32, (dc, LANES), 0) // CROSS_HEAD_DIM
    seg = (head == lax.broadcasted_iota(I32, (dc, LANES), 1)).astype(BF16)
    kv_spec = pl.BlockSpec((None, bs, n_mem, dc), lambda i: (layer, i, 0, 0))
    return pl.pallas_call(
        _cross_sample_body,
        grid=(bsz // bs,),
        in_specs=[pl.BlockSpec((bs, dc), lambda i: (i, 0)), kv_spec, kv_spec,
                  pl.BlockSpec((dc, LANES), lambda i: (0, 0)),
                  pl.BlockSpec((LANES, dc), lambda i: (0, 0))],
        out_specs=pl.BlockSpec((bs, dc), lambda i: (i, 0)),
        out_shape=jax.ShapeDtypeStruct((bsz, dc), BF16),
        compiler_params=_params(("parallel",)),
        name="cross_sample",
    )(q, cache_k, cache_v, seg, seg.T)


def _softmax_step(s, v, m, l, acc):
    m_new = jnp.maximum(m, jnp.max(s, axis=-1, keepdims=True))
    corr = jnp.exp(m - m_new)
    p = jnp.exp(s - m_new)
    l = l * corr + jnp.sum(p, axis=-1, keepdims=True)
    acc = acc * corr + jnp.dot(p.astype(BF16), v, preferred_element_type=F32)
    return m_new, l, acc


def _diff_prompt_body(lam_ref, q_ref, k_ref, v_ref, g_ref, o_ref, *, tq, lambda_init):
    qi = pl.program_id(2)
    q = q_ref[...]
    lane = lax.broadcasted_iota(I32, q.shape, 1)
    zero = jnp.zeros_like(q)
    q1 = jnp.where(lane < DIFF_HEAD_DIM, q, zero)
    q2 = jnp.where(lane >= DIFF_HEAD_DIM, q, zero)
    scale = DIFF_HEAD_DIM ** -0.5

    def chunk(kc, carry, masked):
        off = pl.multiple_of(kc * tq, tq)
        k = k_ref[pl.ds(off, tq), :]
        v = v_ref[pl.ds(off, tq), :]
        s1 = _nt_dot(q1, k) * scale
        s2 = _nt_dot(q2, k) * scale
        if masked:
            keep = (lax.broadcasted_iota(I32, s1.shape, 1) <= lax.broadcasted_iota(I32, s1.shape, 0))
            s1 = jnp.where(keep, s1, -jnp.inf)
            s2 = jnp.where(keep, s2, -jnp.inf)
        m1, l1, a1, m2, l2, a2 = carry
        m1, l1, a1 = _softmax_step(s1, v, m1, l1, a1)
        m2, l2, a2 = _softmax_step(s2, v, m2, l2, a2)
        return m1, l1, a1, m2, l2, a2

    col = lambda val: jnp.full((tq, 1), val, F32)
    acc0 = jnp.zeros((tq, DIFF_V_DIM), F32)
    carry = (col(-jnp.inf), col(0.0), acc0, col(-jnp.inf), col(0.0), acc0)
    carry = lax.fori_loop(0, qi, lambda kc, cr: chunk(kc, cr, False), carry)
    m1, l1, a1, m2, l2, a2 = chunk(qi, carry, True)
    w = a1 / l1 - lam_ref[0] * (a2 / l2)
    o_ref[...] = (_rms(w, g_ref[...]) * (1.0 - lambda_init)).astype(o_ref.dtype)


def _diff_prompt(q, k, v, lam, subln, bsz, seq, lambda_init):
    n_heads = q.shape[1] // DIFF_V_DIM
    tq = min(seq, 512)
    nq = seq // tq
    kv_spec = pl.BlockSpec((seq, DIFF_V_DIM), lambda b, h, i: (b, h))
    q_spec = pl.BlockSpec((tq, DIFF_V_DIM), lambda b, h, i: (b * nq + i, h))
    return pl.pallas_call(
        functools.partial(_diff_prompt_body, tq=tq, lambda_init=lambda_init),
        grid=(bsz, n_heads, nq),
        in_specs=[pl.BlockSpec(memory_space=pltpu.SMEM), q_spec, kv_spec, kv_spec,
                  pl.BlockSpec((1, DIFF_V_DIM), lambda b, h, i: (0, 0))],
        out_specs=q_spec,
        out_shape=jax.ShapeDtypeStruct(q.shape, BF16),
        compiler_params=_params(("parallel", "parallel", "arbitrary")),
        name="diff_prompt",
    )(lam, q, k, v, subln)


def _diff_sample_body(pt_ref, lam_ref, q_ref, kn_ref, vn_ref, g_ref, *rest, n_heads, lambda_init):
    k_refs = rest[:PAGES_PER_STEP]
    v_refs = rest[PAGES_PER_STEP:2 * PAGES_PER_STEP]
    o_ref, qb_ref, m_ref, l_ref, acc_ref = rest[2 * PAGES_PER_STEP:]
    step = pl.program_id(1)
    n_rows = 2 * n_heads
    width = n_heads * DIFF_V_DIM
    row = lax.broadcasted_iota(I32, (n_rows, width), 0)
    col = lax.broadcasted_iota(I32, (n_rows, width), 1)
    own = (col // DIFF_V_DIM == row % n_heads) & ((col // DIFF_HEAD_DIM) % 2 == row // n_heads)

    @pl.when(step == 0)
    def _():
        qs = q_ref[...] * DIFF_HEAD_DIM ** -0.5
        qb_ref[...] = jnp.where(own, jnp.broadcast_to(qs, (n_rows, width)), 0.0).astype(BF16)
        m_ref[...] = jnp.full(m_ref.shape, -jnp.inf, F32)
        l_ref[...] = jnp.zeros(l_ref.shape, F32)
        acc_ref[...] = jnp.zeros(acc_ref.shape, F32)

    qb = qb_ref[...]
    m, l, acc = m_ref[...], l_ref[...], acc_ref[...]
    for k_ref, v_ref in zip(k_refs, v_refs):
        s = _nt_dot(qb, k_ref[...].astype(BF16))
        m, l, acc = _softmax_step(s, v_ref[...].astype(BF16), m, l, acc)
    m_ref[...], l_ref[...], acc_ref[...] = m, l, acc

    @pl.when(step == pl.num_programs(1) - 1)
    def _():
        s_new = jnp.sum(qb.astype(F32) * kn_ref[...], axis=-1, keepdims=True)
        m_new = jnp.maximum(m, s_new)
        corr = jnp.exp(m - m_new)
        p = jnp.exp(s_new - m_new)
        l_fin = l * corr + p
        o = (acc * corr + p * vn_ref[...]) / l_fin
        w = o[:n_heads] - lam_ref[0] * o[n_heads:]
        head_blk = (lax.broadcasted_iota(I32, w.shape, 1) // DIFF_V_DIM
                    == lax.broadcasted_iota(I32, w.shape, 0))
        w = jnp.where(head_blk, w, 0.0)
        ms = jnp.sum(w * w, axis=-1, keepdims=True) / DIFF_V_DIM
        w = jnp.sum(w * lax.rsqrt(ms + EPS), axis=0, keepdims=True)
        o_ref[...] = (w * g_ref[...] * (1.0 - lambda_init)).astype(o_ref.dtype)


def _diff_sample(q, k_new, v_new, cache_k, cache_v, layer, page_table, lam, subln, lambda_init):
    bsz, width = q.shape
    n_heads = width // DIFF_V_DIM
    page = cache_k.shape[2]
    n_pages = page_table.shape[1]
    n_steps = n_pages // PAGES_PER_STEP
    assert n_pages % PAGES_PER_STEP == 0
    row_spec = pl.BlockSpec((None, 1, width), lambda b, s, pt: (b, 0, 0))

    def page_spec(k):
        return pl.BlockSpec((None, None, page, width),
                            lambda b, s, pt, k=k: (layer, pt[b * n_pages + s * PAGES_PER_STEP + k], 0, 0))

    in_specs = [pl.BlockSpec(memory_space=pltpu.SMEM), row_spec, row_spec, row_spec,
                pl.BlockSpec((1, width), lambda b, s, pt: (0, 0))]
    in_specs += [page_spec(k) for k in range(PAGES_PER_STEP)] * 2
    grid_spec = pltpu.PrefetchScalarGridSpec(
        num_scalar_prefetch=1,
        grid=(bsz, n_steps),
        in_specs=in_specs,
        out_specs=row_spec,
        scratch_shapes=[pltpu.VMEM((2 * n_heads, width), BF16),
                        pltpu.VMEM((2 * n_heads, 1), F32),
                        pltpu.VMEM((2 * n_heads, 1), F32),
                        pltpu.VMEM((2 * n_heads, width), F32)],
    )
    r3 = lambda a: a.reshape(bsz, 1, width)
    out = pl.pallas_call(
        functools.partial(_diff_sample_body, n_heads=n_heads, lambda_init=lambda_init),
        grid_spec=grid_spec,
        out_shape=jax.ShapeDtypeStruct((bsz, 1, width), BF16),
        compiler_params=_params(("parallel", "arbitrary")),
        name="diff_sample",
    )(page_table.reshape(-1), lam, r3(q), r3(k_new), r3(v_new), jnp.tile(subln, (1, n_heads)),
      *([cache_k] * PAGES_PER_STEP), *([cache_v] * PAGES_PER_STEP))
    return out.reshape(bsz, width)


def _split_bf16(v):
    hi = v.astype(BF16)
    return hi, (v - hi.astype(F32)).astype(BF16)


def _router_body(xp_ref, xs_ref, g_ref, w_ref, b_ref, hf_ref, eid_ref, gate_ref, *, n_prompt_tiles):
    i = pl.program_id(0)
    x = jnp.where(i < n_prompt_tiles, xp_ref[...], xs_ref[...])
    hf = _rms(x, g_ref[...])
    hf_ref[...] = hf
    h_hi, h_lo = _split_bf16(hf)
    w_hi, w_lo = _split_bf16(w_ref[...])
    dot = lambda a, b: jnp.dot(a, b, preferred_element_type=F32)
    logits = dot(h_hi, w_hi) + (dot(h_lo, w_hi) + dot(h_hi, w_lo)) + b_ref[...]
    lane = lax.broadcasted_iota(I32, logits.shape, 1)
    lane_f = lane.astype(F32)
    first = lambda hit: jnp.min(jnp.where(hit, lane_f, float(LANES)), axis=-1, keepdims=True).astype(I32)
    gl = jnp.where(lane < N_GROUPS, logits, -jnp.inf)
    g_max = jnp.max(gl, axis=-1, keepdims=True)
    g_idx = first(gl == g_max)
    g_w = 1.0 / jnp.sum(jnp.exp(gl - g_max), axis=-1, keepdims=True)
    lo = N_GROUPS + EXPERTS_PER_GROUP * g_idx
    in_group = (lane >= lo) & (lane < lo + EXPERTS_PER_GROUP)
    el = jnp.where(in_group, logits, -jnp.inf)
    ee = jnp.exp(el - jnp.max(el, axis=-1, keepdims=True))
    prob = jnp.where(in_group, ee / jnp.sum(ee, axis=-1, keepdims=True), -1.0)
    p1 = jnp.max(prob, axis=-1, keepdims=True)
    i1 = first(prob == p1)
    rest = jnp.where(lane == i1, -1.0, prob)
    p2 = jnp.max(rest, axis=-1, keepdims=True)
    i2 = first(rest == p2)
    denom = p1 + p2
    eid_ref[...] = jnp.where(lane == 0, i1 - N_GROUPS, jnp.where(lane == 1, i2 - N_GROUPS, 0))
    gate_ref[...] = jnp.where(lane == 0, g_w * p1 / denom, jnp.where(lane == 1, g_w * p2 / denom, 0.0))


def _router(xp, xs, g, w_cat, b_cat):
    d = xp.shape[1]
    t = MOE_TOK_TILE
    n_p, n_s = xp.shape[0] // t, xs.shape[0] // t
    n = xp.shape[0] + xs.shape[0]
    tile = lambda cols: pl.BlockSpec((t, cols), lambda i: (i, 0))
    return pl.pallas_call(
        functools.partial(_router_body, n_prompt_tiles=n_p),
        grid=(n_p + n_s,),
        in_specs=[pl.BlockSpec((t, d), lambda i: (jnp.minimum(i, n_p - 1), 0)),
                  pl.BlockSpec((t, d), lambda i: (jnp.maximum(i - n_p, 0), 0)),
                  pl.BlockSpec((1, d), lambda i: (0, 0)),
                  pl.BlockSpec((d, LANES), lambda i: (0, 0)),
                  pl.BlockSpec((1, LANES), lambda i: (0, 0))],
        out_specs=[tile(d), tile(LANES), tile(LANES)],
        out_shape=[jax.ShapeDtypeStruct((n, d), F32),
                   jax.ShapeDtypeStruct((n, LANES), I32),
                   jax.ShapeDtypeStruct((n, LANES), F32)],
        compiler_params=_params(("parallel",)),
        name="moe_router",
    )(xp, xs, g, w_cat, b_cat)


def _gather_body(tok_ref, nblk_ref, src_ref, o_ref, sem):
    blk = pl.program_id(0)
    rows = o_ref.shape[0]

    def row_copy(r):
        tok = tok_ref[blk * rows + r]
        return pltpu.make_async_copy(src_ref.at[pl.ds(tok, 1)], o_ref.at[pl.ds(r, 1)], sem)

    @pl.when(blk < nblk_ref[0])
    def _():
        def start(r, c):
            row_copy(r).start()
            return c

        def wait(r, c):
            row_copy(r).wait()
            return c

        lax.fori_loop(0, rows, start, 0)
        lax.fori_loop(0, rows, wait, 0)

    @pl.when(blk >= nblk_ref[0])
    def _():
        o_ref[...] = jnp.zeros(o_ref.shape, o_ref.dtype)


def _gather_rows(src, slot_tok, n_used):
    d = src.shape[1]
    n_slots = slot_tok.shape[0]
    grid_spec = pltpu.PrefetchScalarGridSpec(
        num_scalar_prefetch=2,
        grid=(n_slots // MOE_ROWS,),
        in_specs=[pl.BlockSpec(memory_space=pl.ANY)],
        out_specs=pl.BlockSpec((MOE_ROWS, d), lambda b, tok, nb: (b, 0)),
        scratch_shapes=[pltpu.SemaphoreType.DMA(())],
    )
    return pl.pallas_call(
        _gather_body,
        grid_spec=grid_spec,
        out_shape=jax.ShapeDtypeStruct((n_slots, d), src.dtype),
        compiler_params=_params(("arbitrary",)),
        name="moe_gather",
    )(slot_tok, n_used, src)


def _experts_body(be_ref, nblk_ref, x_ref, wg_ref, wu_ref, wd_ref, o_ref, xb_ref):
    blk = pl.program_id(0)
    c = pl.program_id(1)

    @pl.when(blk < nblk_ref[0])
    def _():
        @pl.when(c == 0)
        def _():
            xb_ref[...] = x_ref[...].astype(BF16)

        xb = xb_ref[...]
        gate = jnp.dot(xb, wg_ref[...].astype(BF16), preferred_element_type=F32)
        up = jnp.dot(xb, wu_ref[...].astype(BF16), preferred_element_type=F32)
        h = (jax.nn.silu(gate) * up).astype(BF16)
        y = jnp.dot(h, wd_ref[...].astype(BF16), preferred_element_type=F32)

        @pl.when(c == 0)
        def _():
            o_ref[...] = y

        @pl.when(c > 0)
        def _():
            o_ref[...] += y

    @pl.when((blk >= nblk_ref[0]) & (c == 0))
    def _():
        o_ref[...] = jnp.zeros(o_ref.shape, o_ref.dtype)


def _experts(x_sorted, block_e, n_used, w_gate, w_up, w_down, layer):
    n_slots, d = x_sorted.shape
    de = w_gate.shape[3]
    ck = 256
    n_c = de // ck
    n_blocks = n_slots // MOE_ROWS

    def blk_of(b, nb):
        return jnp.minimum(b, nb[0] - 1)

    grid_spec = pltpu.PrefetchScalarGridSpec(
        num_scalar_prefetch=2,
        grid=(n_blocks, n_c),
        in_specs=[pl.BlockSpec((MOE_ROWS, d), lambda b, c, be, nb: (blk_of(b, nb), 0)),
                  pl.BlockSpec((None, None, d, ck), lambda b, c, be, nb: (layer, be[blk_of(b, nb)], 0, c)),
                  pl.BlockSpec((None, None, d, ck), lambda b, c, be, nb: (layer, be[blk_of(b, nb)], 0, c)),
                  pl.BlockSpec((None, None, ck, d), lambda b, c, be, nb: (layer, be[blk_of(b, nb)], c, 0))],
        out_specs=pl.BlockSpec((MOE_ROWS, d), lambda b, c, be, nb: (b, 0)),
        scratch_shapes=[pltpu.VMEM((MOE_ROWS, d), BF16)],
    )
    return pl.pallas_call(
        _experts_body,
        grid_spec=grid_spec,
        out_shape=jax.ShapeDtypeStruct((n_slots, d), F32),
        compiler_params=_params(("arbitrary", "arbitrary")),
        name="moe_experts",
    )(block_e, n_used, x_sorted, w_gate, w_up, w_down)


def _combine_body(dest_ref, xp_ref, xs_ref, gate_ref, y_ref, g_ref, *rest, n_prompt_tiles, final_norm):
    if final_norm:
        op_ref, os_ref, buf_ref, sem = rest
    else:
        o_ref, buf_ref, sem = rest
    i = pl.program_id(0)
    t = buf_ref.shape[1]

    def row_copy(r, k):
        slot = dest_ref[(i * t + r) * 2 + k]
        return pltpu.make_async_copy(y_ref.at[pl.ds(slot, 1)], buf_ref.at[k, pl.ds(r, 1)], sem)

    def start(r, c):
        row_copy(r, 0).start()
        row_copy(r, 1).start()
        return c

    def wait(r, c):
        row_copy(r, 0).wait()
        row_copy(r, 1).wait()
        return c

    lax.fori_loop(0, t, start, 0)
    lax.fori_loop(0, t, wait, 0)
    x = jnp.where(i < n_prompt_tiles, xp_ref[...], xs_ref[...])
    gates = gate_ref[...]
    out = x + (buf_ref[0] * gates[:, 0:1] + buf_ref[1] * gates[:, 1:2])
    if final_norm:
        out = _rms(out, g_ref[...])

        @pl.when(i < n_prompt_tiles)
        def _():
            op_ref[...] = out

        @pl.when(i >= n_prompt_tiles)
        def _():
            os_ref[...] = out
    else:
        o_ref[...] = out


def _combine(xp, xs, gates, y_sorted, dest, g_final, final_norm):
    d = xp.shape[1]
    t = MOE_TOK_TILE
    n_p, n_s = xp.shape[0] // t, xs.shape[0] // t
    n = xp.shape[0] + xs.shape[0]
    p_map = lambda i, dst: (jnp.minimum(i, n_p - 1), 0)
    s_map = lambda i, dst: (jnp.maximum(i - n_p, 0), 0)
    if final_norm:
        out_specs = [pl.BlockSpec((t, d), p_map), pl.BlockSpec((t, d), s_map)]
        out_shape = [jax.ShapeDtypeStruct(xp.shape, F32), jax.ShapeDtypeStruct(xs.shape, F32)]
    else:
        out_specs = pl.BlockSpec((t, d), lambda i, dst: (i, 0))
        out_shape = jax.ShapeDtypeStruct((n, d), F32)
    grid_spec = pltpu.PrefetchScalarGridSpec(
        num_scalar_prefetch=1,
        grid=(n_p + n_s,),
        in_specs=[pl.BlockSpec((t, d), p_map), pl.BlockSpec((t, d), s_map),
                  pl.BlockSpec((t, LANES), lambda i, dst: (i, 0)),
                  pl.BlockSpec(memory_space=pl.ANY),
                  pl.BlockSpec((1, d), lambda i, dst: (0, 0))],
        out_specs=out_specs,
        scratch_shapes=[pltpu.VMEM((2, t, d), F32), pltpu.SemaphoreType.DMA(())],
    )
    return pl.pallas_call(
        functools.partial(_combine_body, n_prompt_tiles=n_p, final_norm=final_norm),
        grid_spec=grid_spec,
        out_shape=out_shape,
        compiler_params=_params(("arbitrary",)),
        name="moe_combine",
    )(dest, xp, xs, gates, y_sorted, g_final)


def _moe(xp, xs, layer, g_ffn, w_group, b_group, w_router, b_router, w_gate, w_up, w_down, g_final, final_norm):
    d = xp.shape[1]
    n_tok = xp.shape[0] + xs.shape[0]
    pad = LANES - N_GROUPS - N_EXPERTS
    w_cat = jnp.concatenate([w_group[layer], w_router[layer], jnp.zeros((d, pad), F32)], axis=1)
    b_cat = jnp.concatenate([b_group[layer], b_router[layer], jnp.zeros((pad,), F32)])[None]
    hf, eid, gates = _router(xp, xs, g_ffn, w_cat, b_cat)

    e_flat = eid[:, :2].reshape(-1)
    n_asg = e_flat.shape[0]
    onehot = (e_flat[:, None] == jnp.arange(N_EXPERTS, dtype=I32)[None, :]).astype(I32)
    csum = jnp.cumsum(onehot, axis=0)
    rank = jnp.sum((csum - onehot) * onehot, axis=1)
    counts = csum[-1]
    padded = ((counts + MOE_ROWS - 1) // MOE_ROWS) * MOE_ROWS
    pends = jnp.cumsum(padded)
    pstarts = pends - padded
    dest = (jnp.sum(onehot * pstarts[None, :], axis=1) + rank).astype(I32)
    n_blocks = -(-n_asg // MOE_ROWS) + N_EXPERTS
    n_slots = n_blocks * MOE_ROWS
    slot_tok = jnp.zeros((n_slots,), I32).at[dest].set(jnp.arange(n_asg, dtype=I32) // 2)
    block_start = jnp.arange(n_blocks, dtype=I32) * MOE_ROWS
    block_e = jnp.minimum(jnp.sum((block_start[:, None] >= pends[None, :]).astype(I32), axis=1),
                          N_EXPERTS - 1).astype(I32)
    n_used = (pends[-1] // MOE_ROWS).astype(I32).reshape(1)

    x_sorted = _gather_rows(hf, slot_tok, n_used)
    y_sorted = _experts(x_sorted, block_e, n_used, w_gate, w_up, w_down, layer)
    return _combine(xp, xs, gates, y_sorted, dest, g_final, final_norm)


def kernel(x_prompt, x_sample, state_conv, state_s5_re, state_s5_im, cache_k, cache_v, cache_mem_k, cache_mem_v, page_table, mem_prompt, norm_mix, norm_cross, norm_ffn, norm_final, w_in_rec, conv_w, ssm_lambda_re, ssm_lambda_im, ssm_log_dt, ssm_b_re, ssm_b_im, ssm_c_re, ssm_c_im, ssm_d, ssm_w_glu, w_out_rec, w_qkv, diff_lambda_q1, diff_lambda_k1, diff_lambda_q2, diff_lambda_k2, diff_subln, w_o_attn, w_cross_q, w_cross_k, w_cross_v, w_cross_o, w_group, b_group, w_router, b_router, w_exp_gate, w_exp_up, w_exp_down):
    bp, seq, d = x_prompt.shape
    bs = x_sample.shape[0]
    depth = norm_mix.shape[0]
    n_p = bp * seq
    d_conv = conv_w.shape[2]
    n_g = ssm_lambda_re.shape[1]
    n_mem = mem_prompt.shape[1]
    d_cross = w_cross_q.shape[2]
    d_attn = w_o_attn.shape[1] if w_o_attn.shape[0] else d
    lane2 = 2 * SSM_STATE

    xc = None
    xp = x_prompt.reshape(n_p, d)
    xs = x_sample.reshape(bs, d)
    mem = mem_prompt.reshape(bp * n_mem, d)

    conv_p, conv_s, s5_p, s5_s = [], [], [], []
    k_p, v_p, k_s, v_s, memk_p, memv_p = [], [], [], [], [], []

    for layer in range(depth):
        g_mix = norm_mix[layer][None]
        src_p, off_p = (xp, 0) if xc is None else (xc, 0)
        src_s, off_s = (xs, 0) if xc is None else (xc, n_p)
        if layer % 2 == 0:
            r = layer // 2
            yc_p, u_p, tails = _rec_in(src_p, off_p, n_p, seq, g_mix, w_in_rec, r, conv_w[r])
            yc_s, u_s, uc_s = _rec_in(src_s, off_s, bs, 1, g_mix, w_in_rec, r, conv_w[r], buf=state_conv[r])
            tps = tails.shape[0] // bp
            conv_p.append(tails[tps - 1::tps])
            conv_s.append(jnp.stack([state_conv[r][:, 1], uc_s], axis=1))
            ssm = (ssm_lambda_re[r], ssm_lambda_im[r], ssm_log_dt[r], ssm_b_re[r], ssm_b_im[r],
                   ssm_c_re[r], ssm_c_im[r], ssm_d[r])
            y_p, h_p = _s5(u_p, bp, seq, _s5_tables(*ssm, S5_CHUNK, 1),
                           jnp.zeros((bp, n_g * lane2), F32), S5_CHUNK, 1)
            h0_s = jnp.concatenate([state_s5_re[r], state_s5_im[r]], axis=-1).reshape(bs, n_g * lane2)
            y_s, h_s = _s5(u_s, bs, 1, _s5_tables(*ssm, 1, SUBLANES), h0_s, 1, SUBLANES)
            s5_p.append(h_p)
            s5_s.append(h_s)
            new = []
            for yc, y, m, src, off in ((yc_p, y_p, n_p, src_p, off_p), (yc_s, y_s, bs, src_s, off_s)):
                y_ssm = _mm([y], [ssm_w_glu], m=m, n=d_conv, layer=r, prologue="gelu", epilogue="glu", e=y,
                            out_dtypes=(BF16,), name="s5_glu")
                new.append(_mm([yc, y_ssm], [w_out_rec, w_out_rec], m=m, n=d, layer=r, w_rows=(0, 1),
                               epilogue="res", e=src, e_off=off, name="rec_out"))
            xp, xs = new
        else:
            a = layer // 2
            lambda_init = 0.8 - 0.6 * math.exp(-0.3 * layer)
            lam = (jnp.exp(jnp.sum(diff_lambda_q1[a] * diff_lambda_k1[a]))
                   - jnp.exp(jnp.sum(diff_lambda_q2[a] * diff_lambda_k2[a])) + lambda_init).reshape(1)
            subln = diff_subln[a][None]
            qkv = []
            for m, src, off in ((n_p, src_p, off_p), (bs, src_s, off_s)):
                sample = m == bs
                proj = functools.partial(_mm, [src], [w_qkv], m=m, n=d_attn, layer=a, x_off=off,
                                         prologue="norm", g=g_mix)
                q = proj(col_off=0, out_dtypes=(F32,) if sample else (BF16,), name="attn_q")
                if sample:
                    k = proj(col_off=d_attn, name="attn_k")
                    v = proj(col_off=2 * d_attn, name="attn_v")
                    qkv.append((q, k, v))
                else:
                    k, kb = proj(col_off=d_attn, out_dtypes=(F32, BF16), name="attn_k")
                    v, vb = proj(col_off=2 * d_attn, out_dtypes=(F32, BF16), name="attn_v")
                    qkv.append((q, k, v, kb, vb))
            q, k, v, kb, vb = qkv[0]
            k_p.append(k)
            v_p.append(v)
            o_p = _diff_prompt(q, kb, vb, lam, subln, bp, seq, lambda_init)
            q, k, v = qkv[1]
            k_s.append(k)
            v_s.append(v)
            n_pool, page = cache_k.shape[1], cache_k.shape[2]
            o_s = _diff_sample(q, k, v, cache_k.reshape(-1, n_pool, page, d_attn),
                               cache_v.reshape(-1, n_pool, page, d_attn), a, page_table, lam, subln, lambda_init)
            xp = _mm([o_p], [w_o_attn], m=n_p, n=d, layer=a, epilogue="res", e=src_p, e_off=off_p, name="attn_out")
            xs = _mm([o_s], [w_o_attn], m=bs, n=d, layer=a, epilogue="res", e=src_s, e_off=off_s, name="attn_out")

        g_cross = norm_cross[layer][None]
        mk = _mm([mem], [w_cross_k], m=bp * n_mem, n=d_cross, layer=layer, name="mem_k")
        mv = _mm([mem], [w_cross_v], m=bp * n_mem, n=d_cross, layer=layer, name="mem_v")
        memk_p.append(mk)
        memv_p.append(mv)
        q_p = _mm([xp], [w_cross_q], m=n_p, n=d_cross, layer=layer, prologue="norm", g=g_cross,
                  out_dtypes=(BF16,), name="cross_q")
        o_p = _cross_prompt(q_p, mk, mv, bp, seq)
        xp = _mm([o_p], [w_cross_o], m=n_p, n=d, layer=layer, epilogue="res", e=xp, name="cross_out")
        q_s = _mm([xs], [w_cross_q], m=bs, n=d_cross, layer=layer, prologue="norm", g=g_cross, name="cross_q")
        o_s = _cross_sample(q_s, cache_mem_k.reshape(depth, bs, n_mem, d_cross),
                            cache_mem_v.reshape(depth, bs, n_mem, d_cross), layer)
        xs = _mm([o_s], [w_cross_o], m=bs, n=d, layer=layer, epilogue="res", e=xs, name="cross_out")

        last = layer == depth - 1
        out = _moe(xp, xs, layer, norm_ffn[layer][None], w_group, b_group, w_router, b_router,
                   w_exp_gate, w_exp_up, w_exp_down, norm_final[None], last)
        if last:
            xp, xs = out
        else:
            xc = out

    n_h = d_attn // DIFF_V_DIM
    n_ch = d_cross // CROSS_HEAD_DIM
    unpack = lambda hs, b: (jnp.stack([h[..., :SSM_STATE] for h in hs]), jnp.stack([h[..., SSM_STATE:] for h in hs]))
    s5re_p, s5im_p = unpack(s5_p, bp)
    s5re_s, s5im_s = unpack(s5_s, bs)
    return (xp.reshape(bp, seq, d), xs.reshape(bs, 1, d),
            jnp.stack(conv_p), jnp.stack(conv_s),
            s5re_p, s5im_p, s5re_s, s5im_s,
            jnp.stack(k_p).reshape(-1, bp, seq, n_h, 2, DIFF_HEAD_DIM),
            jnp.stack(v_p).reshape(-1, bp, seq, n_h, DIFF_V_DIM),
            jnp.stack(k_s).reshape(-1, bs, 1, n_h, 2, DIFF_HEAD_DIM),
            jnp.stack(v_s).reshape(-1, bs, 1, n_h, DIFF_V_DIM),
            jnp.stack(memk_p).reshape(depth, bp, n_mem, n_ch, CROSS_HEAD_DIM),
            jnp.stack(memv_p).reshape(depth, bp, n_mem, n_ch, CROSS_HEAD_DIM))
```

```python
import functools
import math

import jax
import jax.numpy as jnp
from jax import lax
from jax.experimental import pallas as pl
from jax.experimental.pallas import tpu as pltpu

F32 = jnp.float32
BF16 = jnp.bfloat16
I32 = jnp.int32
EPS = 1e-6

LANES = 128
SUBLANES = 8
VMEM_LIMIT_BYTES = 56 * 1024 * 1024

SSM_GROUP = 16
SSM_STATE = 64
S5_CHUNK = 8
DIFF_HEAD_DIM = 64
DIFF_V_DIM = 128
CROSS_HEAD_DIM = 128
N_GROUPS = 4
EXPERTS_PER_GROUP = 8
N_EXPERTS = N_GROUPS * EXPERTS_PER_GROUP
MOE_ROWS = 512
MOE_TOK_TILE = 128
PAGES_PER_STEP = 4


def _params(semantics):
    return pltpu.CompilerParams(dimension_semantics=semantics, vmem_limit_bytes=VMEM_LIMIT_BYTES)


def _rms(v, g):
    return v * lax.rsqrt(jnp.mean(v * v, axis=-1, keepdims=True) + EPS) * g


def _nt_dot(a, b):
    return lax.dot_general(a, b, (((1,), (1,)), ((), ())), preferred_element_type=F32)


def _mm_body(*refs, n_x, prologue, epilogue, n_out, direct):
    x_refs = refs[:n_x]
    w_refs = refs[n_x:2 * n_x]
    pos = 2 * n_x
    g_ref = e_ref = None
    if prologue == "norm":
        g_ref = refs[pos]
        pos += 1
    if epilogue is not None:
        e_ref = refs[pos]
        pos += 1
    o_refs = refs[pos:pos + n_out]
    xb_refs = refs[pos + n_out:]

    if not direct:
        @pl.when(pl.program_id(1) == 0)
        def _():
            for x_ref, xb_ref in zip(x_refs, xb_refs):
                v = x_ref[...].astype(F32)
                if prologue == "norm":
                    v = _rms(v, g_ref[...])
                elif prologue == "gelu":
                    v = jax.nn.gelu(v)
                xb_ref[...] = v.astype(BF16)

    acc = None
    for k in range(n_x):
        lhs = x_refs[k][...] if direct else xb_refs[k][...]
        d = jnp.dot(lhs, w_refs[k][...].astype(BF16), preferred_element_type=F32)
        acc = d if acc is None else acc + d
    if epilogue == "res":
        acc = acc + e_ref[...]
    elif epilogue == "glu":
        s = jax.nn.gelu(e_ref[...])
        acc = s * jax.nn.sigmoid(acc)
    for o_ref in o_refs:
        o_ref[...] = acc.astype(o_ref.dtype)


def _mm(xs, ws, *, m, n, layer=0, w_rows=None, col_off=0, x_off=0, prologue=None, g=None,
        epilogue=None, e=None, e_off=0, out_dtypes=(F32,), name="mm"):
    n_x = len(xs)
    w_rows = w_rows or (0,) * n_x
    tm = min(m, 1024)
    tn = min(n, 512 if m > 128 else 1024)
    assert m % tm == 0 and n % tn == 0 and col_off % tn == 0 and x_off % tm == 0 and e_off % tm == 0
    xo, eo, co = x_off // tm, e_off // tm, col_off // tn
    direct = prologue is None and all(x.dtype == BF16 for x in xs)

    in_specs, args = [], []
    for x in xs:
        in_specs.append(pl.BlockSpec((tm, x.shape[1]), lambda i, j: (i + xo, 0)))
        args.append(x)
    for x, w, rb in zip(xs, ws, w_rows):
        in_specs.append(pl.BlockSpec((None, x.shape[1], tn), lambda i, j, rb=rb: (layer, rb, j + co)))
        args.append(w)
    if prologue == "norm":
        in_specs.append(pl.BlockSpec((1, g.shape[1]), lambda i, j: (0, 0)))
        args.append(g)
    if epilogue is not None:
        in_specs.append(pl.BlockSpec((tm, tn), lambda i, j: (i + eo, j)))
        args.append(e)
    out_specs = [pl.BlockSpec((tm, tn), lambda i, j: (i, j)) for _ in out_dtypes]
    out_shape = [jax.ShapeDtypeStruct((m, n), dt) for dt in out_dtypes]
    scratch = [] if direct else [pltpu.VMEM((tm, x.shape[1]), BF16) for x in xs]
    outs = pl.pallas_call(
        functools.partial(_mm_body, n_x=n_x, prologue=prologue, epilogue=epilogue,
                          n_out=len(out_dtypes), direct=direct),
        grid=(m // tm, n // tn),
        in_specs=in_specs, out_specs=out_specs, out_shape=out_shape,
        scratch_shapes=scratch,
        compiler_params=_params(("parallel", "arbitrary")),
        name=name,
    )(*args)
    return outs[0] if len(out_dtypes) == 1 else outs


def _rec_in_body(x_ref, g_ref, wv_ref, wc_ref, wb_ref, wu_ref, cw_ref, *rest, sample, tiles_per_seq):
    if sample:
        b0_ref, b1_ref, yc_ref, u_ref, uc_ref, xb_ref = rest
    else:
        yc_ref, u_ref, tail_ref, xb_ref, carry_ref = rest
    i = pl.program_id(0)
    j = pl.program_id(1)

    @pl.when(j == 0)
    def _():
        xb_ref[...] = _rms(x_ref[...], g_ref[...]).astype(BF16)

    xb = xb_ref[...]

    def proj(w_ref):
        return jnp.dot(xb, w_ref[...].astype(BF16), preferred_element_type=F32)

    v = proj(wv_ref)
    c = proj(wc_ref)
    b = proj(wb_ref)
    u_ref[...] = proj(wu_ref)
    uc = c * v
    cw = cw_ref[...]
    if sample:
        y = cw[0:1] * b0_ref[...] + cw[1:2] * b1_ref[...] + cw[2:3] * uc
        uc_ref[...] = uc
    else:
        tm = uc.shape[0]

        @pl.when(i % tiles_per_seq == 0)
        def _():
            carry_ref[j, 0:2, :] = jnp.zeros((2, uc.shape[1]), F32)

        prev = carry_ref[j, 0:2, :]
        row = lax.broadcasted_iota(I32, uc.shape, 0)
        s1 = jnp.where(row == 0, prev[1:2], pltpu.roll(uc, 1, 0))
        s2 = jnp.where(row == 0, prev[0:1], jnp.where(row == 1, prev[1:2], pltpu.roll(uc, 2, 0)))
        y = cw[0:1] * s2 + cw[1:2] * s1 + cw[2:3] * uc
        tail = uc[tm - 2:tm, :]
        tail_ref[...] = tail
        carry_ref[j, 0:2, :] = tail
    yc_ref[...] = (b * y).astype(yc_ref.dtype)


def _rec_in(x, x_off, m, seq, g, w_in, layer, conv_w, buf=None):
    d = x.shape[1]
    c = conv_w.shape[1]
    sample = buf is not None
    tm = min(m, 1024)
    tn = 256
    nj = c // tn
    xo = x_off // tm
    assert m % tm == 0 and x_off % tm == 0 and (sample or seq % tm == 0)
    in_specs = [pl.BlockSpec((tm, d), lambda i, j: (i + xo, 0)),
                pl.BlockSpec((1, d), lambda i, j: (0, 0))]
    for k in range(4):
        in_specs.append(pl.BlockSpec((None, d, tn), lambda i, j, k=k: (layer, 0, k * nj + j)))
    in_specs.append(pl.BlockSpec((3, tn), lambda i, j: (0, j)))
    args = [x, g, w_in, w_in, w_in, w_in, conv_w]
    blk = pl.BlockSpec((tm, tn), lambda i, j: (i, j))
    out_specs = [blk, blk]
    out_shape = [jax.ShapeDtypeStruct((m, c), BF16), jax.ShapeDtypeStruct((m, c), F32)]
    scratch = [pltpu.VMEM((tm, d), BF16)]
    if sample:
        in_specs += [blk, blk]
        args += [buf[:, 0], buf[:, 1]]
        out_specs.append(blk)
        out_shape.append(jax.ShapeDtypeStruct((m, c), F32))
    else:
        out_specs.append(pl.BlockSpec((None, 2, tn), lambda i, j: (i, 0, j)))
        out_shape.append(jax.ShapeDtypeStruct((m // tm, 2, c), F32))
        scratch.append(pltpu.VMEM((nj, SUBLANES, tn), F32))
    return pl.pallas_call(
        functools.partial(_rec_in_body, sample=sample, tiles_per_seq=max(seq // tm, 1)),
        grid=(m // tm, nj),
        in_specs=in_specs, out_specs=out_specs, out_shape=out_shape,
        scratch_shapes=scratch,
        compiler_params=_params(("arbitrary", "arbitrary")),
        name="rec_in_sample" if sample else "rec_in_prompt",
    )(*args)


def _s5_tables(lam_re, lam_im, log_dt, b_re, b_im, c_re, c_im, d_skip, chunk, gs):
    n_g, n_s = lam_re.shape
    dt = jnp.exp(log_dt.astype(F32))[:, None]
    ar, ai = lam_re.astype(F32) * dt, lam_im.astype(F32) * dt
    k = jnp.arange(chunk + 1, dtype=F32)[:, None, None]
    mag = jnp.exp(k * ar)
    pw_re, pw_im = mag * jnp.cos(k * ai), mag * jnp.sin(k * ai)
    xr, xi = pw_re[1] - 1.0, pw_im[1]
    den = lam_re * lam_re + lam_im * lam_im
    qr, qi = (xr * lam_re + xi * lam_im) / den, (xi * lam_re - xr * lam_im) / den
    bb_re = qr[..., None] * b_re - qi[..., None] * b_im
    bb_im = qr[..., None] * b_im + qi[..., None] * b_re
    cp_re = c_re[None] * pw_re[:, :, None, :] - c_im[None] * pw_im[:, :, None, :]
    cp_im = c_re[None] * pw_im[:, :, None, :] + c_im[None] * pw_re[:, :, None, :]
    kern = jnp.sum(cp_re[:chunk, :, :, :, None] * bb_re[None, :, None, :, :]
                   - cp_im[:chunk, :, :, :, None] * bb_im[None, :, None, :, :], axis=3)
    pos = jnp.arange(chunk)
    lag = pos[None, :] - pos[:, None]
    toe = jnp.where((lag >= 0)[:, :, None, None, None], kern[jnp.clip(lag, 0)], 0.0)
    a_t = jnp.transpose(toe, (2, 0, 4, 1, 3))
    e_t = jnp.concatenate([jnp.transpose(cp_re[1:], (1, 3, 0, 2)),
                           -jnp.transpose(cp_im[1:], (1, 3, 0, 2))], axis=1)
    rev_re, rev_im = pw_re[chunk - 1::-1][:chunk], pw_im[chunk - 1::-1][:chunk]
    pb_re = rev_re[..., None] * bb_re[None] - rev_im[..., None] * bb_im[None]
    pb_im = rev_re[..., None] * bb_im[None] + rev_im[..., None] * bb_re[None]
    s_t = jnp.concatenate([jnp.transpose(pb_re, (1, 0, 3, 2)),
                           jnp.transpose(pb_im, (1, 0, 3, 2))], axis=-1)
    lam_a = jnp.concatenate([pw_re[chunk], pw_re[chunk]], axis=-1)
    lam_b = jnp.concatenate([-pw_im[chunk], pw_im[chunk]], axis=-1)

    n_sg = n_g // gs
    s_t, a_t, e_t = lax.optimization_barrier((s_t, a_t, e_t))
    eye = jnp.eye(gs, dtype=F32)
    wd = chunk * gs * SSM_GROUP
    sd = gs * 2 * n_s
    s5 = jnp.transpose(s_t.reshape(n_sg, gs, chunk, SSM_GROUP, 2 * n_s), (0, 2, 1, 3, 4))
    s_bd = s5[:, :, :, :, None, :] * eye[None, None, :, None, :, None]
    a6 = jnp.transpose(a_t.reshape(n_sg, gs, chunk, SSM_GROUP, chunk, SSM_GROUP), (0, 2, 1, 3, 4, 5))
    a_bd = a6[:, :, :, :, :, None, :] * eye[None, None, :, None, None, :, None]
    e5 = e_t.reshape(n_sg, gs, 2 * n_s, chunk, SSM_GROUP)
    e_bd = e5[:, :, :, :, None, :] * eye[None, :, None, None, :, None]
    return (s_bd.reshape(n_sg, wd, sd).astype(BF16), a_bd.reshape(n_sg, wd, wd).astype(BF16),
            e_bd.reshape(n_sg, sd, wd).astype(BF16), lam_a, lam_b, d_skip.astype(F32).reshape(1, -1))


def _s5_local_body(u_ref, st_ref, o_ref):
    o_ref[...] = jnp.dot(u_ref[...].astype(BF16), st_ref[...], preferred_element_type=F32)


def _s5_scan_body(s_ref, h0_ref, la_ref, lb_ref, hc_ref, hl_ref, *, n_chunks):
    la = la_ref[...]
    lb = lb_ref[...]

    def step(c, h):
        hc_ref[c] = h
        return la * h + lb * pltpu.roll(h, SSM_STATE, 1) + s_ref[c]

    hl_ref[...] = lax.fori_loop(0, n_chunks, step, h0_ref[...])


def _s5_out_body(u_ref, at_ref, hc_ref, et_ref, d_ref, y_ref):
    u = u_ref[...]
    y = jnp.dot(u.astype(BF16), at_ref[...], preferred_element_type=F32)
    y = y + jnp.dot(hc_ref[...].astype(BF16), et_ref[...], preferred_element_type=F32)
    y_ref[...] = y + u * d_ref[...]


def _s5_prompt_body(u_ref, s_ref, a_ref, e_ref, la_ref, lb_ref, d_ref, y_ref, hl_ref,
                    ucat_ref, st_ref, hc_ref, *, chunk, n_c):
    gs = hl_ref.shape[0]
    for p in range(chunk):
        ucat_ref[:, p * LANES:(p + 1) * LANES] = u_ref[pl.ds(p, n_c, stride=chunk), :].astype(BF16)
    uc = ucat_ref[...]
    s_loc = jnp.dot(uc, s_ref[...], preferred_element_type=F32)
    for g in range(gs):
        st_ref[pl.ds(g, n_c, stride=gs), :] = s_loc[:, g * LANES:(g + 1) * LANES]
    la = la_ref[...]
    lb = lb_ref[...]

    def step(c, h):
        off = pl.multiple_of(c * gs, gs)
        s = st_ref[pl.ds(off, gs), :]
        st_ref[pl.ds(off, gs), :] = h
        return la * h + lb * pltpu.roll(h, SSM_STATE, 1) + s

    hl_ref[...] = lax.fori_loop(0, n_c, step, jnp.zeros((gs, LANES), F32))
    for g in range(gs):
        hc_ref[:, g * LANES:(g + 1) * LANES] = st_ref[pl.ds(g, n_c, stride=gs), :].astype(BF16)
    y = jnp.dot(uc, a_ref[...], preferred_element_type=F32)
    y = y + jnp.dot(hc_ref[...], e_ref[...], preferred_element_type=F32)
    d = d_ref[...]
    for p in range(chunk):
        rows = pl.ds(p, n_c, stride=chunk)
        y_ref[rows, :] = y[:, p * LANES:(p + 1) * LANES] + u_ref[rows, :] * d


def _s5_prompt(u, bsz, seq, tables, chunk):
    s_bd, a_bd, e_bd, lam_a, lam_b, d = tables
    n_g = u.shape[1] // SSM_GROUP
    gs = LANES // SSM_GROUP
    n_blk = n_g // gs
    n_c = seq // chunk
    wd = chunk * LANES
    sd = gs * 2 * SSM_STATE
    tok = pl.BlockSpec((seq, LANES), lambda k, b: (b, k))
    wspec = lambda r, c: pl.BlockSpec((None, r, c), lambda k, b: (k, 0, 0))
    lam_spec = pl.BlockSpec((gs, 2 * SSM_STATE), lambda k, b: (k, 0))
    return pl.pallas_call(
        functools.partial(_s5_prompt_body, chunk=chunk, n_c=n_c),
        grid=(n_blk, bsz),
        in_specs=[tok, wspec(wd, sd), wspec(wd, wd), wspec(sd, wd), lam_spec, lam_spec,
                  pl.BlockSpec((1, LANES), lambda k, b: (0, k))],
        out_specs=[tok, pl.BlockSpec((None, gs, 2 * SSM_STATE), lambda k, b: (b, k, 0))],
        out_shape=[jax.ShapeDtypeStruct(u.shape, F32),
                   jax.ShapeDtypeStruct((bsz, n_g, 2 * SSM_STATE), F32)],
        scratch_shapes=[pltpu.VMEM((n_c, wd), BF16),
                        pltpu.VMEM((n_c * gs, 2 * SSM_STATE), F32),
                        pltpu.VMEM((n_c, sd), BF16)],
        compiler_params=_params(("parallel", "parallel")),
        name="s5_prompt",
    )(u, s_bd, a_bd, e_bd, lam_a, lam_b, d)


def _s5_step(u, tables, h0, gs):
    s_t, a_t, e_t, lam_a, lam_b, d_t = tables
    bsz, seq, chunk = u.shape[0], 1, 1
    n_g = u.shape[1] // SSM_GROUP
    n_sg = n_g // gs
    n_c = 1
    rows = bsz
    w = gs * SSM_GROUP
    sw = gs * 2 * SSM_STATE
    up = u

    s_loc = pl.pallas_call(
        _s5_local_body,
        grid=(n_sg,),
        in_specs=[pl.BlockSpec((rows, w), lambda s: (0, s)),
                  pl.BlockSpec((None, w, sw), lambda s: (s, 0, 0))],
        out_specs=pl.BlockSpec((rows, sw), lambda s: (0, s)),
        out_shape=jax.ShapeDtypeStruct((rows, n_g * 2 * SSM_STATE), F32),
        compiler_params=_params(("parallel",)),
        name="s5_local",
    )(up, s_t)

    r2 = bsz * n_g
    rb = min(r2, 1024 if n_c == 1 else 64)
    lane = 2 * SSM_STATE
    la = jnp.broadcast_to(lam_a[None], (bsz, n_g, lane)).reshape(r2, lane)
    lb = jnp.broadcast_to(lam_b[None], (bsz, n_g, lane)).reshape(r2, lane)
    hc, h_last = pl.pallas_call(
        functools.partial(_s5_scan_body, n_chunks=n_c),
        grid=(r2 // rb,),
        in_specs=[pl.BlockSpec((n_c, rb, lane), lambda r: (0, r, 0)),
                  pl.BlockSpec((rb, lane), lambda r: (r, 0)),
                  pl.BlockSpec((rb, lane), lambda r: (r, 0)),
                  pl.BlockSpec((rb, lane), lambda r: (r, 0))],
        out_specs=[pl.BlockSpec((n_c, rb, lane), lambda r: (0, r, 0)),
                   pl.BlockSpec((rb, lane), lambda r: (r, 0))],
        out_shape=[jax.ShapeDtypeStruct((n_c, r2, lane), F32),
                   jax.ShapeDtypeStruct((r2, lane), F32)],
        compiler_params=_params(("parallel",)),
        name="s5_scan",
    )(s_loc.reshape(n_c, r2, lane), h0.reshape(r2, lane), la, lb)

    yp = pl.pallas_call(
        _s5_out_body,
        grid=(n_sg,),
        in_specs=[pl.BlockSpec((rows, w), lambda s: (0, s)),
                  pl.BlockSpec((None, w, w), lambda s: (s, 0, 0)),
                  pl.BlockSpec((rows, sw), lambda s: (0, s)),
                  pl.BlockSpec((None, sw, w), lambda s: (s, 0, 0)),
                  pl.BlockSpec((1, w), lambda s: (0, s))],
        out_specs=pl.BlockSpec((rows, w), lambda s: (0, s)),
        out_shape=jax.ShapeDtypeStruct((rows, n_g * chunk * SSM_GROUP), F32),
        compiler_params=_params(("parallel",)),
        name="s5_out",
    )(up, a_t, hc.reshape(rows, n_g * lane), e_t, d_t)
    return yp, h_last.reshape(bsz, n_g, lane)


def _cross_prompt_body(q_ref, k_ref, v_ref, o_ref, *, n_heads):
    scale = CROSS_HEAD_DIM ** -0.5
    for h in range(n_heads):
        sl = slice(h * CROSS_HEAD_DIM, (h + 1) * CROSS_HEAD_DIM)
        s = _nt_dot(q_ref[:, sl], k_ref[:, sl].astype(BF16)) * scale
        e = jnp.exp(s - jnp.max(s, axis=-1, keepdims=True))
        l = jnp.sum(e, axis=-1, keepdims=True)
        o = jnp.dot(e.astype(BF16), v_ref[:, sl].astype(BF16), preferred_element_type=F32)
        o_ref[:, sl] = (o / l).astype(o_ref.dtype)


def _cross_prompt(q, mem_k, mem_v, bsz, seq):
    dc = q.shape[1]
    n_mem = mem_k.shape[0] // bsz
    tq = min(seq, 1024)
    nq = seq // tq
    return pl.pallas_call(
        functools.partial(_cross_prompt_body, n_heads=dc // CROSS_HEAD_DIM),
        grid=(bsz, nq),
        in_specs=[pl.BlockSpec((tq, dc), lambda b, i: (b * nq + i, 0)),
                  pl.BlockSpec((n_mem, dc), lambda b, i: (b, 0)),
                  pl.BlockSpec((n_mem, dc), lambda b, i: (b, 0))],
        out_specs=pl.BlockSpec((tq, dc), lambda b, i: (b * nq + i, 0)),
        out_shape=jax.ShapeDtypeStruct((bsz * seq, dc), BF16),
        compiler_params=_params(("parallel", "parallel")),
        name="cross_prompt",
    )(q, mem_k, mem_v)


def _cross_sample_body(q_ref, k_ref, v_ref, ones_ref, o_ref):
    bs, rows, dh = k_ref.shape
    n_heads = q_ref.shape[1]
    scale = CROSS_HEAD_DIM ** -0.5
    q = q_ref[...]
    row_head = lax.broadcasted_iota(I32, (bs, rows, dh), 1) % n_heads
    qt = jnp.broadcast_to(q[:, 0:1, :], (bs, rows, dh))
    for h in range(1, n_heads):
        qt = jnp.where(row_head == h, q[:, h:h + 1, :], qt)
    kq = (k_ref[...] * qt).reshape(bs * rows, dh).astype(BF16)
    s = jnp.dot(kq, ones_ref[...], preferred_element_type=F32) * scale
    s = s.reshape(bs, rows // SUBLANES, SUBLANES, dh)

    def over_memory(x, reduce, combine):
        r = reduce(x, axis=1)
        shift = n_heads
        while shift < SUBLANES:
            r = combine(r, pltpu.roll(r, shift, 1))
            shift *= 2
        return r

    m = over_memory(s, jnp.max, jnp.maximum)
    e = jnp.exp(s - m[:, None])
    p = e / over_memory(e, jnp.sum, jnp.add)[:, None]
    o_ref[...] = over_memory(p * v_ref[...].reshape(s.shape), jnp.sum, jnp.add)


def _cross_sample(q, cache_k, cache_v, layer):
    bsz, dc = q.shape
    n_heads = dc // CROSS_HEAD_DIM
    rows = cache_k.shape[2]
    bs = SUBLANES
    kv_spec = pl.BlockSpec((None, bs, rows, CROSS_HEAD_DIM), lambda i: (layer, i, 0, 0))
    out = pl.pallas_call(
        _cross_sample_body,
        grid=(bsz // bs,),
        in_specs=[pl.BlockSpec((bs, n_heads, CROSS_HEAD_DIM), lambda i: (i, 0, 0)), kv_spec, kv_spec,
                  pl.BlockSpec((CROSS_HEAD_DIM, LANES), lambda i: (0, 0))],
        out_specs=pl.BlockSpec((bs, SUBLANES, CROSS_HEAD_DIM), lambda i: (i, 0, 0)),
        out_shape=jax.ShapeDtypeStruct((bsz, SUBLANES, CROSS_HEAD_DIM), F32),
        compiler_params=_params(("parallel",)),
        name="cross_sample",
    )(q.reshape(bsz, n_heads, CROSS_HEAD_DIM), cache_k, cache_v, jnp.ones((CROSS_HEAD_DIM, LANES), BF16))
    return out[:, :n_heads].reshape(bsz, dc).astype(BF16)


def _softmax_step(s, v, m, l, acc):
    m_new = jnp.maximum(m, jnp.max(s, axis=-1, keepdims=True))
    corr = jnp.exp(m - m_new)
    p = jnp.exp(s - m_new)
    l = l * corr + jnp.sum(p, axis=-1, keepdims=True)
    acc = acc * corr + jnp.dot(p.astype(BF16), v, preferred_element_type=F32)
    return m_new, l, acc


def _diff_prompt_body(lam_ref, q_ref, k_ref, v_ref, g_ref, o_ref, *, tq, lambda_init):
    qi = pl.program_id(2)
    q = q_ref[...]
    lane = lax.broadcasted_iota(I32, q.shape, 1)
    zero = jnp.zeros_like(q)
    q1 = jnp.where(lane < DIFF_HEAD_DIM, q, zero)
    q2 = jnp.where(lane >= DIFF_HEAD_DIM, q, zero)
    scale = DIFF_HEAD_DIM ** -0.5

    def chunk(kc, carry, masked):
        off = pl.multiple_of(kc * tq, tq)
        k = k_ref[pl.ds(off, tq), :]
        v = v_ref[pl.ds(off, tq), :]
        s1 = _nt_dot(q1, k) * scale
        s2 = _nt_dot(q2, k) * scale
        if masked:
            keep = (lax.broadcasted_iota(I32, s1.shape, 1) <= lax.broadcasted_iota(I32, s1.shape, 0))
            s1 = jnp.where(keep, s1, -jnp.inf)
            s2 = jnp.where(keep, s2, -jnp.inf)
        m1, l1, a1, m2, l2, a2 = carry
        m1, l1, a1 = _softmax_step(s1, v, m1, l1, a1)
        m2, l2, a2 = _softmax_step(s2, v, m2, l2, a2)
        return m1, l1, a1, m2, l2, a2

    col = lambda val: jnp.full((tq, 1), val, F32)
    acc0 = jnp.zeros((tq, DIFF_V_DIM), F32)
    carry = (col(-jnp.inf), col(0.0), acc0, col(-jnp.inf), col(0.0), acc0)
    carry = lax.fori_loop(0, qi, lambda kc, cr: chunk(kc, cr, False), carry)
    m1, l1, a1, m2, l2, a2 = chunk(qi, carry, True)
    w = a1 / l1 - lam_ref[0] * (a2 / l2)
    o_ref[...] = (_rms(w, g_ref[...]) * (1.0 - lambda_init)).astype(o_ref.dtype)


def _diff_prompt(q, k, v, lam, subln, bsz, seq, lambda_init):
    n_heads = q.shape[1] // DIFF_V_DIM
    tq = min(seq, 512)
    nq = seq // tq
    kv_spec = pl.BlockSpec((seq, DIFF_V_DIM), lambda b, h, i: (b, h))
    q_spec = pl.BlockSpec((tq, DIFF_V_DIM), lambda b, h, i: (b * nq + i, h))
    return pl.pallas_call(
        functools.partial(_diff_prompt_body, tq=tq, lambda_init=lambda_init),
        grid=(bsz, n_heads, nq),
        in_specs=[pl.BlockSpec(memory_space=pltpu.SMEM), q_spec, kv_spec, kv_spec,
                  pl.BlockSpec((1, DIFF_V_DIM), lambda b, h, i: (0, 0))],
        out_specs=q_spec,
        out_shape=jax.ShapeDtypeStruct(q.shape, BF16),
        compiler_params=_params(("parallel", "parallel", "arbitrary")),
        name="diff_prompt",
    )(lam, q, k, v, subln)


def _diff_sample_body(pt_ref, lam_ref, q_ref, kn_ref, vn_ref, g_ref, x_ref, *rest, n_heads, lambda_init):
    k_refs = rest[:PAGES_PER_STEP]
    v_refs = rest[PAGES_PER_STEP:2 * PAGES_PER_STEP]
    o_ref, qb_ref, m_ref, l_ref, acc_ref = rest[2 * PAGES_PER_STEP:]
    step = pl.program_id(1)
    n_rows = 2 * n_heads
    width = n_heads * DIFF_V_DIM
    row = lax.broadcasted_iota(I32, (n_rows, width), 0)
    col = lax.broadcasted_iota(I32, (n_rows, width), 1)
    own = (col // DIFF_V_DIM == row % n_heads) & ((col // DIFF_HEAD_DIM) % 2 == row // n_heads)

    @pl.when(step == 0)
    def _():
        qs = q_ref[...] * DIFF_HEAD_DIM ** -0.5
        qb_ref[...] = jnp.where(own, jnp.broadcast_to(qs, (n_rows, width)), 0.0).astype(BF16)
        m_ref[...] = jnp.full(m_ref.shape, -jnp.inf, F32)
        l_ref[...] = jnp.zeros(l_ref.shape, F32)
        acc_ref[...] = jnp.zeros(acc_ref.shape, F32)

    qb = qb_ref[...]
    m, l, acc = m_ref[...], l_ref[...], acc_ref[...]
    pe_shape = (n_rows, x_ref.shape[1])
    mine = (lax.broadcasted_iota(I32, pe_shape, 1) % n_heads
            == lax.broadcasted_iota(I32, pe_shape, 0) % n_heads)
    for k_ref, v_ref in zip(k_refs, v_refs):
        s = jnp.dot(qb, k_ref[...].astype(BF16), preferred_element_type=F32)
        m_new = jnp.maximum(m, jnp.max(s, axis=-1, keepdims=True))
        corr = jnp.exp(m - m_new)
        p = jnp.exp(s - m_new)
        l = l * corr + jnp.sum(p, axis=-1, keepdims=True)
        pe = jnp.dot(p.astype(BF16), x_ref[...], preferred_element_type=F32)
        pe = jnp.where(mine, pe, 0.0).astype(BF16)
        acc = acc * corr + jnp.dot(pe, v_ref[...].astype(BF16), preferred_element_type=F32)
        m = m_new
    m_ref[...], l_ref[...], acc_ref[...] = m, l, acc

    @pl.when(step == pl.num_programs(1) - 1)
    def _():
        s_new = jnp.sum(qb.astype(F32) * kn_ref[...], axis=-1, keepdims=True)
        m_new = jnp.maximum(m, s_new)
        corr = jnp.exp(m - m_new)
        p = jnp.exp(s_new - m_new)
        l_fin = l * corr + p
        vn = vn_ref[...]
        o = (acc * corr + p * jnp.concatenate([vn, vn], axis=0)) / l_fin
        w = o[:n_heads] - lam_ref[0] * o[n_heads:]
        o_ref[...] = (_rms(w, g_ref[...]) * (1.0 - lambda_init)).astype(o_ref.dtype)


def _diff_sample(q, k_new, v_new, cache_kt, cache_v, layer, page_table, lam, subln, lambda_init):
    bsz, width = q.shape
    n_heads = width // DIFF_V_DIM
    page = cache_kt.shape[3]
    n_pages = page_table.shape[1]
    n_steps = n_pages // PAGES_PER_STEP
    assert n_pages % PAGES_PER_STEP == 0
    row_spec = pl.BlockSpec((None, 1, width), lambda b, s, pt: (b, 0, 0))
    head_spec = pl.BlockSpec((None, n_heads, DIFF_V_DIM), lambda b, s, pt: (b, 0, 0))

    def page_spec(k, rows, cols):
        return pl.BlockSpec((None, None, rows, cols),
                            lambda b, s, pt, k=k: (layer, pt[b * n_pages + s * PAGES_PER_STEP + k], 0, 0))

    expand = (lax.broadcasted_iota(I32, (page, page * n_heads), 1) // n_heads
              == lax.broadcasted_iota(I32, (page, page * n_heads), 0)).astype(BF16)
    in_specs = [pl.BlockSpec(memory_space=pltpu.SMEM), row_spec, row_spec, head_spec,
                pl.BlockSpec((1, DIFF_V_DIM), lambda b, s, pt: (0, 0)),
                pl.BlockSpec((page, page * n_heads), lambda b, s, pt: (0, 0))]
    in_specs += [page_spec(k, width, page) for k in range(PAGES_PER_STEP)]
    in_specs += [page_spec(k, page * n_heads, DIFF_V_DIM) for k in range(PAGES_PER_STEP)]
    grid_spec = pltpu.PrefetchScalarGridSpec(
        num_scalar_prefetch=1,
        grid=(bsz, n_steps),
        in_specs=in_specs,
        out_specs=head_spec,
        scratch_shapes=[pltpu.VMEM((2 * n_heads, width), BF16),
                        pltpu.VMEM((2 * n_heads, 1), F32),
                        pltpu.VMEM((2 * n_heads, 1), F32),
                        pltpu.VMEM((2 * n_heads, DIFF_V_DIM), F32)],
    )
    out = pl.pallas_call(
        functools.partial(_diff_sample_body, n_heads=n_heads, lambda_init=lambda_init),
        grid_spec=grid_spec,
        out_shape=jax.ShapeDtypeStruct((bsz, n_heads, DIFF_V_DIM), BF16),
        compiler_params=_params(("parallel", "arbitrary")),
        name="diff_sample",
    )(page_table.reshape(-1), lam, q.reshape(bsz, 1, width), k_new.reshape(bsz, 1, width),
      v_new.reshape(bsz, n_heads, DIFF_V_DIM), subln, expand,
      *([cache_kt] * PAGES_PER_STEP), *([cache_v] * PAGES_PER_STEP))
    return out.reshape(bsz, width)


def _split_bf16(v):
    hi = v.astype(BF16)
    return hi, (v - hi.astype(F32)).astype(BF16)


def _to_slabs(slab_ref, value):
    t, width = value.shape
    n_slab = width // LANES
    for s in range(n_slab):
        slab_ref[pl.ds(s, t, stride=n_slab), :] = value[:, s * LANES:(s + 1) * LANES].astype(slab_ref.dtype)


def _slab_column(slab_ref, s, t, n_slab, first_row=0):
    return slab_ref[pl.ds(first_row + s, t, stride=n_slab), :]


def _start_slab_copies(src_ref, dst_ref, sem, index_of, n_copies, n_slab, unroll=8):
    def body(it, carry):
        for u in range(unroll):
            r = it * unroll + u
            src = pl.multiple_of(index_of(r) * n_slab, n_slab)
            dst = pl.multiple_of(r * n_slab, n_slab)
            pltpu.make_async_copy(src_ref.at[pl.ds(src, n_slab)], dst_ref.at[pl.ds(dst, n_slab)], sem).start()
        return carry

    lax.fori_loop(0, n_copies // unroll, body, 0)


def _wait_slab_copies(src_ref, dst_ref, sem):
    pltpu.make_async_copy(src_ref.at[pl.ds(0, dst_ref.shape[0])], dst_ref, sem).wait()


def _router_body(xp_ref, xs_ref, g_ref, w_ref, b_ref, hf_ref, eid_ref, gate_ref, *, n_prompt_tiles):
    i = pl.program_id(0)
    x = jnp.where(i < n_prompt_tiles, xp_ref[...], xs_ref[...])
    hf = _rms(x, g_ref[...])
    _to_slabs(hf_ref, hf)
    h_hi, h_lo = _split_bf16(hf)
    w_hi, w_lo = _split_bf16(w_ref[...])
    dot = lambda a, b: jnp.dot(a, b, preferred_element_type=F32)
    logits = dot(h_hi, w_hi) + (dot(h_lo, w_hi) + dot(h_hi, w_lo)) + b_ref[...]
    lane = lax.broadcasted_iota(I32, logits.shape, 1)
    lane_f = lane.astype(F32)
    first = lambda hit: jnp.min(jnp.where(hit, lane_f, float(LANES)), axis=-1, keepdims=True).astype(I32)
    gl = jnp.where(lane < N_GROUPS, logits, -jnp.inf)
    g_max = jnp.max(gl, axis=-1, keepdims=True)
    g_idx = first(gl == g_max)
    g_w = 1.0 / jnp.sum(jnp.exp(gl - g_max), axis=-1, keepdims=True)
    lo = N_GROUPS + EXPERTS_PER_GROUP * g_idx
    in_group = (lane >= lo) & (lane < lo + EXPERTS_PER_GROUP)
    el = jnp.where(in_group, logits, -jnp.inf)
    ee = jnp.exp(el - jnp.max(el, axis=-1, keepdims=True))
    prob = jnp.where(in_group, ee / jnp.sum(ee, axis=-1, keepdims=True), -1.0)
    p1 = jnp.max(prob, axis=-1, keepdims=True)
    i1 = first(prob == p1)
    rest = jnp.where(lane == i1, -1.0, prob)
    p2 = jnp.max(rest, axis=-1, keepdims=True)
    i2 = first(rest == p2)
    denom = p1 + p2
    eid_ref[...] = jnp.where(lane == 0, i1 - N_GROUPS, jnp.where(lane == 1, i2 - N_GROUPS, 0))
    gate_ref[...] = jnp.where(lane == 0, g_w * p1 / denom, jnp.where(lane == 1, g_w * p2 / denom, 0.0))


def _router(xp, xs, g, w_cat, b_cat):
    d = xp.shape[1]
    t = MOE_TOK_TILE
    n_p, n_s = xp.shape[0] // t, xs.shape[0] // t
    n = xp.shape[0] + xs.shape[0]
    tile = lambda cols: pl.BlockSpec((t, cols), lambda i: (i, 0))
    return pl.pallas_call(
        functools.partial(_router_body, n_prompt_tiles=n_p),
        grid=(n_p + n_s,),
        in_specs=[pl.BlockSpec((t, d), lambda i: (jnp.minimum(i, n_p - 1), 0)),
                  pl.BlockSpec((t, d), lambda i: (jnp.maximum(i - n_p, 0), 0)),
                  pl.BlockSpec((1, d), lambda i: (0, 0)),
                  pl.BlockSpec((d, LANES), lambda i: (0, 0)),
                  pl.BlockSpec((1, LANES), lambda i: (0, 0))],
        out_specs=[pl.BlockSpec((t * (d // LANES), LANES), lambda i: (i, 0)), tile(LANES), tile(LANES)],
        out_shape=[jax.ShapeDtypeStruct((n * (d // LANES), LANES), F32),
                   jax.ShapeDtypeStruct((n, LANES), I32),
                   jax.ShapeDtypeStruct((n, LANES), F32)],
        compiler_params=_params(("parallel",)),
        name="moe_router",
    )(xp, xs, g, w_cat, b_cat)


def _gather_body(tok_ref, nblk_ref, src_ref, o_ref, sem, *, n_slab):
    blk = pl.program_id(0)
    rows = o_ref.shape[0] // n_slab

    @pl.when(blk < nblk_ref[0])
    def _():
        _start_slab_copies(src_ref, o_ref, sem, lambda r: tok_ref[blk * rows + r], rows, n_slab)
        _wait_slab_copies(src_ref, o_ref, sem)

    @pl.when(blk >= nblk_ref[0])
    def _():
        o_ref[...] = jnp.zeros(o_ref.shape, o_ref.dtype)


def _gather_rows(src, slot_tok, n_used, n_slab):
    n_slots = slot_tok.shape[0]
    grid_spec = pltpu.PrefetchScalarGridSpec(
        num_scalar_prefetch=2,
        grid=(n_slots // MOE_ROWS,),
        in_specs=[pl.BlockSpec(memory_space=pl.ANY)],
        out_specs=pl.BlockSpec((MOE_ROWS * n_slab, LANES), lambda b, tok, nb: (b, 0)),
        scratch_shapes=[pltpu.SemaphoreType.DMA(())],
    )
    return pl.pallas_call(
        functools.partial(_gather_body, n_slab=n_slab),
        grid_spec=grid_spec,
        out_shape=jax.ShapeDtypeStruct((n_slots * n_slab, LANES), src.dtype),
        compiler_params=_params(("arbitrary",)),
        name="moe_gather",
    )(slot_tok, n_used, src)


def _experts_body(be_ref, nblk_ref, x_ref, wg_ref, wu_ref, wd_ref, o_ref, xb_ref, acc_ref):
    blk = pl.program_id(0)
    c = pl.program_id(1)
    rows, d = xb_ref.shape
    n_slab = d // LANES

    @pl.when(blk < nblk_ref[0])
    def _():
        @pl.when(c == 0)
        def _():
            for s in range(n_slab):
                xb_ref[:, s * LANES:(s + 1) * LANES] = _slab_column(x_ref, s, rows, n_slab).astype(BF16)

        xb = xb_ref[...]
        gate = jnp.dot(xb, wg_ref[...].astype(BF16), preferred_element_type=F32)
        up = jnp.dot(xb, wu_ref[...].astype(BF16), preferred_element_type=F32)
        h = (jax.nn.silu(gate) * up).astype(BF16)
        y = jnp.dot(h, wd_ref[...].astype(BF16), preferred_element_type=F32)

        @pl.when(c == 0)
        def _():
            acc_ref[...] = y

        @pl.when((c > 0) & (c < pl.num_programs(1) - 1))
        def _():
            acc_ref[...] += y

        @pl.when(c == pl.num_programs(1) - 1)
        def _():
            _to_slabs(o_ref, acc_ref[...] + y)

    @pl.when((blk >= nblk_ref[0]) & (c == 0))
    def _():
        o_ref[...] = jnp.zeros(o_ref.shape, o_ref.dtype)


def _experts(x_sorted, block_e, n_used, w_gate, w_up, w_down, layer):
    d, de = w_gate.shape[2], w_gate.shape[3]
    n_slab = d // LANES
    n_slots = x_sorted.shape[0] // n_slab
    ck = 256
    n_c = de // ck
    n_blocks = n_slots // MOE_ROWS
    assert n_c > 1

    def blk_of(b, nb):
        return jnp.minimum(b, nb[0] - 1)

    grid_spec = pltpu.PrefetchScalarGridSpec(
        num_scalar_prefetch=2,
        grid=(n_blocks, n_c),
        in_specs=[pl.BlockSpec((MOE_ROWS * n_slab, LANES), lambda b, c, be, nb: (blk_of(b, nb), 0)),
                  pl.BlockSpec((None, None, d, ck), lambda b, c, be, nb: (layer, be[blk_of(b, nb)], 0, c)),
                  pl.BlockSpec((None, None, d, ck), lambda b, c, be, nb: (layer, be[blk_of(b, nb)], 0, c)),
                  pl.BlockSpec((None, None, ck, d), lambda b, c, be, nb: (layer, be[blk_of(b, nb)], c, 0))],
        out_specs=pl.BlockSpec((MOE_ROWS * n_slab, LANES), lambda b, c, be, nb: (b, 0)),
        scratch_shapes=[pltpu.VMEM((MOE_ROWS, d), BF16), pltpu.VMEM((MOE_ROWS, d), F32)],
    )
    return pl.pallas_call(
        _experts_body,
        grid_spec=grid_spec,
        out_shape=jax.ShapeDtypeStruct((n_slots * n_slab, LANES), F32),
        compiler_params=_params(("arbitrary", "arbitrary")),
        name="moe_experts",
    )(block_e, n_used, x_sorted, w_gate, w_up, w_down)


def _combine_body(dest_ref, xp_ref, xs_ref, gate_ref, y_ref, g_ref, *rest, n_prompt_tiles, final_norm):
    if final_norm:
        op_ref, os_ref, buf_ref, sem = rest
    else:
        o_ref, buf_ref, sem = rest
    i = pl.program_id(0)
    t, d = xp_ref.shape
    n_slab = d // LANES

    _start_slab_copies(y_ref, buf_ref, sem, lambda a: dest_ref[i * 2 * t + a], 2 * t, n_slab)
    _wait_slab_copies(y_ref, buf_ref, sem)
    x = jnp.where(i < n_prompt_tiles, xp_ref[...], xs_ref[...])
    gates = gate_ref[...]
    g0, g1 = gates[:, 0:1], gates[:, 1:2]
    pieces = []
    for s in range(n_slab):
        y0 = _slab_column(buf_ref, s, t, 2 * n_slab)
        y1 = _slab_column(buf_ref, s, t, 2 * n_slab, first_row=n_slab)
        pieces.append(x[:, s * LANES:(s + 1) * LANES] + (y0 * g0 + y1 * g1))
    out = jnp.concatenate(pieces, axis=1)
    if final_norm:
        out = _rms(out, g_ref[...])

        @pl.when(i < n_prompt_tiles)
        def _():
            op_ref[...] = out

        @pl.when(i >= n_prompt_tiles)
        def _():
            os_ref[...] = out
    else:
        o_ref[...] = out


def _combine(xp, xs, gates, y_sorted, dest, g_final, final_norm):
    d = xp.shape[1]
    t = MOE_TOK_TILE
    n_p, n_s = xp.shape[0] // t, xs.shape[0] // t
    n = xp.shape[0] + xs.shape[0]
    p_map = lambda i, dst: (jnp.minimum(i, n_p - 1), 0)
    s_map = lambda i, dst: (jnp.maximum(i - n_p, 0), 0)
    if final_norm:
        out_specs = [pl.BlockSpec((t, d), p_map), pl.BlockSpec((t, d), s_map)]
        out_shape = [jax.ShapeDtypeStruct(xp.shape, F32), jax.ShapeDtypeStruct(xs.shape, F32)]
    else:
        out_specs = pl.BlockSpec((t, d), lambda i, dst: (i, 0))
        out_shape = jax.ShapeDtypeStruct((n, d), F32)
    grid_spec = pltpu.PrefetchScalarGridSpec(
        num_scalar_prefetch=1,
        grid=(n_p + n_s,),
        in_specs=[pl.BlockSpec((t, d), p_map), pl.BlockSpec((t, d), s_map),
                  pl.BlockSpec((t, LANES), lambda i, dst: (i, 0)),
                  pl.BlockSpec(memory_space=pl.ANY),
                  pl.BlockSpec((1, d), lambda i, dst: (0, 0))],
        out_specs=out_specs,
        scratch_shapes=[pltpu.VMEM((2 * t * (d // LANES), LANES), F32), pltpu.SemaphoreType.DMA(())],
    )
    return pl.pallas_call(
        functools.partial(_combine_body, n_prompt_tiles=n_p, final_norm=final_norm),
        grid_spec=grid_spec,
        out_shape=out_shape,
        compiler_params=_params(("arbitrary",)),
        name="moe_combine",
    )(dest, xp, xs, gates, y_sorted, g_final)


def _moe(xp, xs, layer, g_ffn, w_group, b_group, w_router, b_router, w_gate, w_up, w_down, g_final, final_norm):
    d = xp.shape[1]
    n_tok = xp.shape[0] + xs.shape[0]
    pad = LANES - N_GROUPS - N_EXPERTS
    w_cat = jnp.concatenate([w_group[layer], w_router[layer], jnp.zeros((d, pad), F32)], axis=1)
    b_cat = jnp.concatenate([b_group[layer], b_router[layer], jnp.zeros((pad,), F32)])[None]
    hf, eid, gates = _router(xp, xs, g_ffn, w_cat, b_cat)

    e_flat = eid[:, :2].reshape(-1)
    n_asg = e_flat.shape[0]
    onehot = (e_flat[:, None] == jnp.arange(N_EXPERTS, dtype=I32)[None, :]).astype(I32)
    csum = jnp.cumsum(onehot, axis=0)
    rank = jnp.sum((csum - onehot) * onehot, axis=1)
    counts = csum[-1]
    padded = ((counts + MOE_ROWS - 1) // MOE_ROWS) * MOE_ROWS
    pends = jnp.cumsum(padded)
    pstarts = pends - padded
    dest = (jnp.sum(onehot * pstarts[None, :], axis=1) + rank).astype(I32)
    n_blocks = -(-n_asg // MOE_ROWS) + N_EXPERTS
    n_slots = n_blocks * MOE_ROWS
    slot_tok = jnp.zeros((n_slots,), I32).at[dest].set(jnp.arange(n_asg, dtype=I32) // 2)
    block_start = jnp.arange(n_blocks, dtype=I32) * MOE_ROWS
    block_e = jnp.minimum(jnp.sum((block_start[:, None] >= pends[None, :]).astype(I32), axis=1),
                          N_EXPERTS - 1).astype(I32)
    n_used = (pends[-1] // MOE_ROWS).astype(I32).reshape(1)

    x_sorted = _gather_rows(hf, slot_tok, n_used, d // LANES)
    y_sorted = _experts(x_sorted, block_e, n_used, w_gate, w_up, w_down, layer)
    return _combine(xp, xs, gates, y_sorted, dest, g_final, final_norm)


def kernel(x_prompt, x_sample, state_conv, state_s5_re, state_s5_im, cache_k, cache_v, cache_mem_k, cache_mem_v, page_table, mem_prompt, norm_mix, norm_cross, norm_ffn, norm_final, w_in_rec, conv_w, ssm_lambda_re, ssm_lambda_im, ssm_log_dt, ssm_b_re, ssm_b_im, ssm_c_re, ssm_c_im, ssm_d, ssm_w_glu, w_out_rec, w_qkv, diff_lambda_q1, diff_lambda_k1, diff_lambda_q2, diff_lambda_k2, diff_subln, w_o_attn, w_cross_q, w_cross_k, w_cross_v, w_cross_o, w_group, b_group, w_router, b_router, w_exp_gate, w_exp_up, w_exp_down):
    bp, seq, d = x_prompt.shape
    bs = x_sample.shape[0]
    depth = norm_mix.shape[0]
    n_p = bp * seq
    d_conv = conv_w.shape[2]
    n_g = ssm_lambda_re.shape[1]
    n_mem = mem_prompt.shape[1]
    d_cross = w_cross_q.shape[2]
    d_attn = w_o_attn.shape[1] if w_o_attn.shape[0] else d
    lane2 = 2 * SSM_STATE

    xc = None
    xp = x_prompt.reshape(n_p, d)
    xs = x_sample.reshape(bs, d)
    mem = mem_prompt.reshape(bp * n_mem, d)

    conv_p, conv_s, s5_p, s5_s = [], [], [], []
    k_p, v_p, k_s, v_s, memk_p, memv_p = [], [], [], [], [], []

    for layer in range(depth):
        g_mix = norm_mix[layer][None]
        src_p, off_p = (xp, 0) if xc is None else (xc, 0)
        src_s, off_s = (xs, 0) if xc is None else (xc, n_p)
        if layer % 2 == 0:
            r = layer // 2
            yc_p, u_p, tails = _rec_in(src_p, off_p, n_p, seq, g_mix, w_in_rec, r, conv_w[r])
            yc_s, u_s, uc_s = _rec_in(src_s, off_s, bs, 1, g_mix, w_in_rec, r, conv_w[r], buf=state_conv[r])
            tps = tails.shape[0] // bp
            conv_p.append(tails[tps - 1::tps])
            conv_s.append(jnp.stack([state_conv[r][:, 1], uc_s], axis=1))
            ssm = (ssm_lambda_re[r], ssm_lambda_im[r], ssm_log_dt[r], ssm_b_re[r], ssm_b_im[r],
                   ssm_c_re[r], ssm_c_im[r], ssm_d[r])
            y_p, h_p = _s5_prompt(u_p, bp, seq, _s5_tables(*ssm, S5_CHUNK, LANES // SSM_GROUP), S5_CHUNK)
            h0_s = jnp.concatenate([state_s5_re[r], state_s5_im[r]], axis=-1).reshape(bs, n_g * lane2)
            y_s, h_s = _s5_step(u_s, _s5_tables(*ssm, 1, SUBLANES), h0_s, SUBLANES)
            s5_p.append(h_p)
            s5_s.append(h_s)
            new = []
            for yc, y, m, src, off in ((yc_p, y_p, n_p, src_p, off_p), (yc_s, y_s, bs, src_s, off_s)):
                y_ssm = _mm([y], [ssm_w_glu], m=m, n=d_conv, layer=r, prologue="gelu", epilogue="glu", e=y,
                            out_dtypes=(BF16,), name="s5_glu")
                new.append(_mm([yc, y_ssm], [w_out_rec, w_out_rec], m=m, n=d, layer=r, w_rows=(0, 1),
                               epilogue="res", e=src, e_off=off, name="rec_out"))
            xp, xs = new
        else:
            a = layer // 2
            lambda_init = 0.8 - 0.6 * math.exp(-0.3 * layer)
            lam = (jnp.exp(jnp.sum(diff_lambda_q1[a] * diff_lambda_k1[a]))
                   - jnp.exp(jnp.sum(diff_lambda_q2[a] * diff_lambda_k2[a])) + lambda_init).reshape(1)
            subln = diff_subln[a][None]
            qkv = []
            for m, src, off in ((n_p, src_p, off_p), (bs, src_s, off_s)):
                sample = m == bs
                proj = functools.partial(_mm, [src], [w_qkv], m=m, n=d_attn, layer=a, x_off=off,
                                         prologue="norm", g=g_mix)
                q = proj(col_off=0, out_dtypes=(F32,) if sample else (BF16,), name="attn_q")
                if sample:
                    k = proj(col_off=d_attn, name="attn_k")
                    v = proj(col_off=2 * d_attn, name="attn_v")
                    qkv.append((q, k, v))
                else:
                    k, kb = proj(col_off=d_attn, out_dtypes=(F32, BF16), name="attn_k")
                    v, vb = proj(col_off=2 * d_attn, out_dtypes=(F32, BF16), name="attn_v")
                    qkv.append((q, k, v, kb, vb))
            q, k, v, kb, vb = qkv[0]
            k_p.append(k)
            v_p.append(v)
            o_p = _diff_prompt(q, kb, vb, lam, subln, bp, seq, lambda_init)
            q, k, v = qkv[1]
            k_s.append(k)
            v_s.append(v)
            n_pool, page = cache_k.shape[1], cache_k.shape[2]
            cache_kt = jnp.transpose(cache_k, (0, 1, 3, 4, 5, 2)).reshape(-1, n_pool, d_attn, page)
            cache_vr = cache_v.reshape(-1, n_pool, page * (d_attn // DIFF_V_DIM), DIFF_V_DIM)
            o_s = _diff_sample(q, k, v, cache_kt, cache_vr, a, page_table, lam, subln, lambda_init)
            xp = _mm([o_p], [w_o_attn], m=n_p, n=d, layer=a, epilogue="res", e=src_p, e_off=off_p, name="attn_out")
            xs = _mm([o_s], [w_o_attn], m=bs, n=d, layer=a, epilogue="res", e=src_s, e_off=off_s, name="attn_out")

        g_cross = norm_cross[layer][None]
        mk = _mm([mem], [w_cross_k], m=bp * n_mem, n=d_cross, layer=layer, name="mem_k")
        mv = _mm([mem], [w_cross_v], m=bp * n_mem, n=d_cross, layer=layer, name="mem_v")
        memk_p.append(mk)
        memv_p.append(mv)
        q_p = _mm([xp], [w_cross_q], m=n_p, n=d_cross, layer=layer, prologue="norm", g=g_cross,
                  out_dtypes=(BF16,), name="cross_q")
        o_p = _cross_prompt(q_p, mk, mv, bp, seq)
        xp = _mm([o_p], [w_cross_o], m=n_p, n=d, layer=layer, epilogue="res", e=xp, name="cross_out")
        q_s = _mm([xs], [w_cross_q], m=bs, n=d_cross, layer=layer, prologue="norm", g=g_cross, name="cross_q")
        mem_rows = n_mem * (d_cross // CROSS_HEAD_DIM)
        o_s = _cross_sample(q_s, cache_mem_k.reshape(depth, bs, mem_rows, CROSS_HEAD_DIM),
                            cache_mem_v.reshape(depth, bs, mem_rows, CROSS_HEAD_DIM), layer)
        xs = _mm([o_s], [w_cross_o], m=bs, n=d, layer=layer, epilogue="res", e=xs, name="cross_out")

        last = layer == depth - 1
        out = _moe(xp, xs, layer, norm_ffn[layer][None], w_group, b_group, w_router, b_router,
                   w_exp_gate, w_exp_up, w_exp_down, norm_final[None], last)
        if last:
            xp, xs = out
        else:
            xc = out

    n_h = d_attn // DIFF_V_DIM
    n_ch = d_cross // CROSS_HEAD_DIM
    unpack = lambda hs, b: (jnp.stack([h[..., :SSM_STATE] for h in hs]), jnp.stack([h[..., SSM_STATE:] for h in hs]))
    s5re_p, s5im_p = unpack(s5_p, bp)
    s5re_s, s5im_s = unpack(s5_s, bs)
    return (xp.reshape(bp, seq, d), xs.reshape(bs, 1, d),
            jnp.stack(conv_p), jnp.stack(conv_s),
            s5re_p, s5im_p, s5re_s, s5im_s,
            jnp.stack(k_p).reshape(-1, bp, seq, n_h, 2, DIFF_HEAD_DIM),
            jnp.stack(v_p).reshape(-1, bp, seq, n_h, DIFF_V_DIM),
            jnp.stack(k_s).reshape(-1, bs, 1, n_h, 2, DIFF_HEAD_DIM),
            jnp.stack(v_s).reshape(-1, bs, 1, n_h, DIFF_V_DIM),
            jnp.stack(memk_p).reshape(depth, bp, n_mem, n_ch, CROSS_HEAD_DIM),
            jnp.stack(memv_p).reshape(depth, bp, n_mem, n_ch, CROSS_HEAD_DIM))
```

```python
import functools
import math

import jax
import jax.numpy as jnp
from jax import lax
from jax.experimental import pallas as pl
from jax.experimental.pallas import tpu as pltpu

F32 = jnp.float32
BF16 = jnp.bfloat16
I32 = jnp.int32
EPS = 1e-6

LANES = 128
SUBLANES = 8
VMEM_LIMIT_BYTES = 56 * 1024 * 1024

SSM_GROUP = 16
SSM_STATE = 64
S5_CHUNK = 8
DIFF_HEAD_DIM = 64
DIFF_V_DIM = 128
CROSS_HEAD_DIM = 128
N_GROUPS = 4
EXPERTS_PER_GROUP = 8
N_EXPERTS = N_GROUPS * EXPERTS_PER_GROUP
MOE_ROWS = 512
MOE_TOK_TILE = 128
PAGES_PER_STEP = 4


def _params(semantics):
    return pltpu.CompilerParams(dimension_semantics=semantics, vmem_limit_bytes=VMEM_LIMIT_BYTES)


def _rms(v, g):
    return v * lax.rsqrt(jnp.mean(v * v, axis=-1, keepdims=True) + EPS) * g


def _nt_dot(a, b):
    return lax.dot_general(a, b, (((1,), (1,)), ((), ())), preferred_element_type=F32)


def _mm_body(*refs, n_x, prologue, epilogue, n_out, direct):
    x_refs = refs[:n_x]
    w_refs = refs[n_x:2 * n_x]
    pos = 2 * n_x
    g_ref = e_ref = None
    if prologue == "norm":
        g_ref = refs[pos]
        pos += 1
    if epilogue is not None:
        e_ref = refs[pos]
        pos += 1
    o_refs = refs[pos:pos + n_out]
    xb_refs = refs[pos + n_out:]

    if not direct:
        @pl.when(pl.program_id(1) == 0)
        def _():
            for x_ref, xb_ref in zip(x_refs, xb_refs):
                v = x_ref[...].astype(F32)
                if prologue == "norm":
                    v = _rms(v, g_ref[...])
                elif prologue == "gelu":
                    v = jax.nn.gelu(v)
                xb_ref[...] = v.astype(BF16)

    acc = None
    for k in range(n_x):
        lhs = x_refs[k][...] if direct else xb_refs[k][...]
        d = jnp.dot(lhs, w_refs[k][...].astype(BF16), preferred_element_type=F32)
        acc = d if acc is None else acc + d
    if epilogue == "res":
        acc = acc + e_ref[...]
    elif epilogue == "glu":
        s = jax.nn.gelu(e_ref[...])
        acc = s * jax.nn.sigmoid(acc)
    for o_ref in o_refs:
        o_ref[...] = acc.astype(o_ref.dtype)


def _mm(xs, ws, *, m, n, layer=0, w_rows=None, col_off=0, x_off=0, prologue=None, g=None,
        epilogue=None, e=None, e_off=0, out_dtypes=(F32,), name="mm"):
    n_x = len(xs)
    w_rows = w_rows or (0,) * n_x
    tm = min(m, 1024)
    tn = min(n, 512 if m > 128 else 1024)
    assert m % tm == 0 and n % tn == 0 and col_off % tn == 0 and x_off % tm == 0 and e_off % tm == 0
    xo, eo, co = x_off // tm, e_off // tm, col_off // tn
    direct = prologue is None and all(x.dtype == BF16 for x in xs)

    in_specs, args = [], []
    for x in xs:
        in_specs.append(pl.BlockSpec((tm, x.shape[1]), lambda i, j: (i + xo, 0)))
        args.append(x)
    for x, w, rb in zip(xs, ws, w_rows):
        in_specs.append(pl.BlockSpec((None, x.shape[1], tn), lambda i, j, rb=rb: (layer, rb, j + co)))
        args.append(w)
    if prologue == "norm":
        in_specs.append(pl.BlockSpec((1, g.shape[1]), lambda i, j: (0, 0)))
        args.append(g)
    if epilogue is not None:
        in_specs.append(pl.BlockSpec((tm, tn), lambda i, j: (i + eo, j)))
        args.append(e)
    out_specs = [pl.BlockSpec((tm, tn), lambda i, j: (i, j)) for _ in out_dtypes]
    out_shape = [jax.ShapeDtypeStruct((m, n), dt) for dt in out_dtypes]
    scratch = [] if direct else [pltpu.VMEM((tm, x.shape[1]), BF16) for x in xs]
    outs = pl.pallas_call(
        functools.partial(_mm_body, n_x=n_x, prologue=prologue, epilogue=epilogue,
                          n_out=len(out_dtypes), direct=direct),
        grid=(m // tm, n // tn),
        in_specs=in_specs, out_specs=out_specs, out_shape=out_shape,
        scratch_shapes=scratch,
        compiler_params=_params(("parallel", "arbitrary")),
        name=name,
    )(*args)
    return outs[0] if len(out_dtypes) == 1 else outs


def _rec_in_body(x_ref, g_ref, wv_ref, wc_ref, wb_ref, wu_ref, cw_ref, *rest, sample, tiles_per_seq):
    if sample:
        b0_ref, b1_ref, yc_ref, u_ref, uc_ref, xb_ref = rest
    else:
        yc_ref, u_ref, tail_ref, xb_ref, carry_ref = rest
    i = pl.program_id(0)
    j = pl.program_id(1)

    @pl.when(j == 0)
    def _():
        xb_ref[...] = _rms(x_ref[...], g_ref[...]).astype(BF16)

    xb = xb_ref[...]

    def proj(w_ref):
        return jnp.dot(xb, w_ref[...].astype(BF16), preferred_element_type=F32)

    v = proj(wv_ref)
    c = proj(wc_ref)
    b = proj(wb_ref)
    u_ref[...] = proj(wu_ref)
    uc = c * v
    cw = cw_ref[...]
    if sample:
        y = cw[0:1] * b0_ref[...] + cw[1:2] * b1_ref[...] + cw[2:3] * uc
        uc_ref[...] = uc
    else:
        tm = uc.shape[0]

        @pl.when(i % tiles_per_seq == 0)
        def _():
            carry_ref[j, 0:2, :] = jnp.zeros((2, uc.shape[1]), F32)

        prev = carry_ref[j, 0:2, :]
        row = lax.broadcasted_iota(I32, uc.shape, 0)
        s1 = jnp.where(row == 0, prev[1:2], pltpu.roll(uc, 1, 0))
        s2 = jnp.where(row == 0, prev[0:1], jnp.where(row == 1, prev[1:2], pltpu.roll(uc, 2, 0)))
        y = cw[0:1] * s2 + cw[1:2] * s1 + cw[2:3] * uc
        tail = uc[tm - 2:tm, :]
        tail_ref[...] = tail
        carry_ref[j, 0:2, :] = tail
    yc_ref[...] = (b * y).astype(yc_ref.dtype)


def _rec_in(x, x_off, m, seq, g, w_in, layer, conv_w, buf=None):
    d = x.shape[1]
    c = conv_w.shape[1]
    sample = buf is not None
    tm = min(m, 1024)
    tn = 256
    nj = c // tn
    xo = x_off // tm
    assert m % tm == 0 and x_off % tm == 0 and (sample or seq % tm == 0)
    in_specs = [pl.BlockSpec((tm, d), lambda i, j: (i + xo, 0)),
                pl.BlockSpec((1, d), lambda i, j: (0, 0))]
    for k in range(4):
        in_specs.append(pl.BlockSpec((None, d, tn), lambda i, j, k=k: (layer, 0, k * nj + j)))
    in_specs.append(pl.BlockSpec((3, tn), lambda i, j: (0, j)))
    args = [x, g, w_in, w_in, w_in, w_in, conv_w]
    blk = pl.BlockSpec((tm, tn), lambda i, j: (i, j))
    out_specs = [blk, blk]
    out_shape = [jax.ShapeDtypeStruct((m, c), BF16), jax.ShapeDtypeStruct((m, c), F32)]
    scratch = [pltpu.VMEM((tm, d), BF16)]
    if sample:
        in_specs += [blk, blk]
        args += [buf[:, 0], buf[:, 1]]
        out_specs.append(blk)
        out_shape.append(jax.ShapeDtypeStruct((m, c), F32))
    else:
        out_specs.append(pl.BlockSpec((None, 2, tn), lambda i, j: (i, 0, j)))
        out_shape.append(jax.ShapeDtypeStruct((m // tm, 2, c), F32))
        scratch.append(pltpu.VMEM((nj, SUBLANES, tn), F32))
    return pl.pallas_call(
        functools.partial(_rec_in_body, sample=sample, tiles_per_seq=max(seq // tm, 1)),
        grid=(m // tm, nj),
        in_specs=in_specs, out_specs=out_specs, out_shape=out_shape,
        scratch_shapes=scratch,
        compiler_params=_params(("arbitrary", "arbitrary")),
        name="rec_in_sample" if sample else "rec_in_prompt",
    )(*args)


def _s5_block_body(sg_ref, ag_ref, eg_ref, p_ref, q_ref, s_ref, a_ref, e_ref):
    gs = p_ref.shape[0]
    a_acc = None
    for a in range(gs):
        p, q = p_ref[a], q_ref[a]
        rows = jnp.dot(p, ag_ref[a].astype(BF16), preferred_element_type=F32).astype(BF16)
        part = jnp.dot(rows, q, preferred_element_type=F32)
        a_acc = part if a_acc is None else a_acc + part
        s_ref[:, a * LANES:(a + 1) * LANES] = jnp.dot(
            p, sg_ref[a].astype(BF16), preferred_element_type=F32).astype(BF16)
        e_ref[a * LANES:(a + 1) * LANES, :] = jnp.dot(
            eg_ref[a].astype(BF16), q, preferred_element_type=F32).astype(BF16)
    a_ref[...] = a_acc.astype(BF16)


def _s5_block_tables(s_g, a_g, e_g, chunk, gs):
    n_g = s_g.shape[0]
    n_sg = n_g // gs
    wd = chunk * gs * SSM_GROUP
    r = lax.broadcasted_iota(I32, (gs, wd, LANES), 1)
    q = lax.broadcasted_iota(I32, (gs, wd, LANES), 2)
    a = lax.broadcasted_iota(I32, (gs, wd, LANES), 0)
    place = ((r // SSM_GROUP) % gs == a) & (r // (gs * SSM_GROUP) == q // SSM_GROUP) & (r % SSM_GROUP == q % SSM_GROUP)
    p = place.astype(BF16)
    grp = pl.BlockSpec((gs, LANES, LANES), lambda s: (s, 0, 0))
    out = pl.BlockSpec((None, wd, wd), lambda s: (s, 0, 0))
    shape = jax.ShapeDtypeStruct((n_sg, wd, wd), BF16)
    return pl.pallas_call(
        _s5_block_body,
        grid=(n_sg,),
        in_specs=[grp, grp, grp,
                  pl.BlockSpec((gs, wd, LANES), lambda s: (0, 0, 0)),
                  pl.BlockSpec((gs, LANES, wd), lambda s: (0, 0, 0))],
        out_specs=[out, out, out],
        out_shape=[shape, shape, shape],
        compiler_params=_params(("parallel",)),
        name="s5_tables",
    )(s_g, a_g, e_g, p, jnp.transpose(p, (0, 2, 1)))


def _s5_tables(lam_re, lam_im, log_dt, b_re, b_im, c_re, c_im, d_skip, chunk, gs):
    n_g, n_s = lam_re.shape
    dt = jnp.exp(log_dt.astype(F32))[:, None]
    ar, ai = lam_re.astype(F32) * dt, lam_im.astype(F32) * dt
    k = jnp.arange(chunk + 1, dtype=F32)[:, None, None]
    mag = jnp.exp(k * ar)
    pw_re, pw_im = mag * jnp.cos(k * ai), mag * jnp.sin(k * ai)
    xr, xi = pw_re[1] - 1.0, pw_im[1]
    den = lam_re * lam_re + lam_im * lam_im
    qr, qi = (xr * lam_re + xi * lam_im) / den, (xi * lam_re - xr * lam_im) / den
    bb_re = qr[..., None] * b_re - qi[..., None] * b_im
    bb_im = qr[..., None] * b_im + qi[..., None] * b_re
    cp_re = c_re[None] * pw_re[:, :, None, :] - c_im[None] * pw_im[:, :, None, :]
    cp_im = c_re[None] * pw_im[:, :, None, :] + c_im[None] * pw_re[:, :, None, :]
    kern = jnp.sum(cp_re[:chunk, :, :, :, None] * bb_re[None, :, None, :, :]
                   - cp_im[:chunk, :, :, :, None] * bb_im[None, :, None, :, :], axis=3)
    pos = jnp.arange(chunk)
    lag = pos[None, :] - pos[:, None]
    toe = jnp.where((lag >= 0)[:, :, None, None, None], kern[jnp.clip(lag, 0)], 0.0)
    a_t = jnp.transpose(toe, (2, 0, 4, 1, 3))
    e_t = jnp.concatenate([jnp.transpose(cp_re[1:], (1, 3, 0, 2)),
                           -jnp.transpose(cp_im[1:], (1, 3, 0, 2))], axis=1)
    rev_re, rev_im = pw_re[chunk - 1::-1][:chunk], pw_im[chunk - 1::-1][:chunk]
    pb_re = rev_re[..., None] * bb_re[None] - rev_im[..., None] * bb_im[None]
    pb_im = rev_re[..., None] * bb_im[None] + rev_im[..., None] * bb_re[None]
    s_t = jnp.concatenate([jnp.transpose(pb_re, (1, 0, 3, 2)),
                           jnp.transpose(pb_im, (1, 0, 3, 2))], axis=-1)
    lam_a = jnp.concatenate([pw_re[chunk], pw_re[chunk]], axis=-1)
    lam_b = jnp.concatenate([-pw_im[chunk], pw_im[chunk]], axis=-1)

    d_row = d_skip.astype(F32).reshape(1, -1)
    if chunk * SSM_GROUP == LANES:
        s_bd, a_bd, e_bd = _s5_block_tables(s_t.reshape(n_g, LANES, 2 * n_s), a_t.reshape(n_g, LANES, LANES),
                                            e_t.reshape(n_g, 2 * n_s, LANES), chunk, gs)
        return s_bd, a_bd, e_bd, lam_a, lam_b, d_row

    n_sg = n_g // gs
    s_t, a_t, e_t = lax.optimization_barrier((s_t, a_t, e_t))
    eye = jnp.eye(gs, dtype=F32)
    wd = chunk * gs * SSM_GROUP
    sd = gs * 2 * n_s
    s5 = jnp.transpose(s_t.reshape(n_sg, gs, chunk, SSM_GROUP, 2 * n_s), (0, 2, 1, 3, 4))
    s_bd = s5[:, :, :, :, None, :] * eye[None, None, :, None, :, None]
    a6 = jnp.transpose(a_t.reshape(n_sg, gs, chunk, SSM_GROUP, chunk, SSM_GROUP), (0, 2, 1, 3, 4, 5))
    a_bd = a6[:, :, :, :, :, None, :] * eye[None, None, :, None, None, :, None]
    e5 = e_t.reshape(n_sg, gs, 2 * n_s, chunk, SSM_GROUP)
    e_bd = e5[:, :, :, :, None, :] * eye[None, :, None, None, :, None]
    return (s_bd.reshape(n_sg, wd, sd).astype(BF16), a_bd.reshape(n_sg, wd, wd).astype(BF16),
            e_bd.reshape(n_sg, sd, wd).astype(BF16), lam_a, lam_b, d_skip.astype(F32).reshape(1, -1))


def _s5_local_body(u_ref, st_ref, o_ref):
    o_ref[...] = jnp.dot(u_ref[...].astype(BF16), st_ref[...], preferred_element_type=F32)


def _s5_scan_body(s_ref, h0_ref, la_ref, lb_ref, hc_ref, hl_ref, *, n_chunks):
    la = la_ref[...]
    lb = lb_ref[...]

    def step(c, h):
        hc_ref[c] = h
        return la * h + lb * pltpu.roll(h, SSM_STATE, 1) + s_ref[c]

    hl_ref[...] = lax.fori_loop(0, n_chunks, step, h0_ref[...])


def _s5_out_body(u_ref, at_ref, hc_ref, et_ref, d_ref, y_ref):
    u = u_ref[...]
    y = jnp.dot(u.astype(BF16), at_ref[...], preferred_element_type=F32)
    y = y + jnp.dot(hc_ref[...].astype(BF16), et_ref[...], preferred_element_type=F32)
    y_ref[...] = y + u * d_ref[...]


def _s5_prompt_body(u_ref, s_ref, a_ref, e_ref, la_ref, lb_ref, d_ref, y_ref, hl_ref,
                    ucat_ref, st_ref, sw_ref, hc_ref, *, chunk, n_c):
    gs = hl_ref.shape[0]
    for p in range(chunk):
        ucat_ref[:, p * LANES:(p + 1) * LANES] = u_ref[pl.ds(p, n_c, stride=chunk), :].astype(BF16)
    uc = ucat_ref[...]
    s_loc = jnp.dot(uc, s_ref[...], preferred_element_type=F32)
    for g in range(gs):
        blk = s_loc[:, g * LANES:(g + 1) * LANES]
        st_ref[pl.ds(g, n_c, stride=gs), :] = blk
        sw_ref[pl.ds(g, n_c, stride=gs), :] = pltpu.roll(blk, SSM_STATE, 1)
    la = la_ref[...]
    lb = lb_ref[...]

    def step(c, carry):
        h, hs = carry
        off = pl.multiple_of(c * gs, gs)
        s = st_ref[pl.ds(off, gs), :]
        ss = sw_ref[pl.ds(off, gs), :]
        st_ref[pl.ds(off, gs), :] = h
        return la * h + lb * hs + s, la * hs - lb * h + ss

    zero = jnp.zeros((gs, LANES), F32)
    hl_ref[...] = lax.fori_loop(0, n_c, step, (zero, zero), unroll=8)[0]
    for g in range(gs):
        hc_ref[:, g * LANES:(g + 1) * LANES] = st_ref[pl.ds(g, n_c, stride=gs), :].astype(BF16)
    y = jnp.dot(uc, a_ref[...], preferred_element_type=F32)
    y = y + jnp.dot(hc_ref[...], e_ref[...], preferred_element_type=F32)
    d = d_ref[...]
    for p in range(chunk):
        rows = pl.ds(p, n_c, stride=chunk)
        y_ref[rows, :] = y[:, p * LANES:(p + 1) * LANES] + u_ref[rows, :] * d


def _s5_prompt(u, bsz, seq, tables, chunk):
    s_bd, a_bd, e_bd, lam_a, lam_b, d = tables
    n_g = u.shape[1] // SSM_GROUP
    gs = LANES // SSM_GROUP
    n_blk = n_g // gs
    n_c = seq // chunk
    wd = chunk * LANES
    sd = gs * 2 * SSM_STATE
    tok = pl.BlockSpec((seq, LANES), lambda k, b: (b, k))
    wspec = lambda r, c: pl.BlockSpec((None, r, c), lambda k, b: (k, 0, 0))
    lam_spec = pl.BlockSpec((gs, 2 * SSM_STATE), lambda k, b: (k, 0))
    return pl.pallas_call(
        functools.partial(_s5_prompt_body, chunk=chunk, n_c=n_c),
        grid=(n_blk, bsz),
        in_specs=[tok, wspec(wd, sd), wspec(wd, wd), wspec(sd, wd), lam_spec, lam_spec,
                  pl.BlockSpec((1, LANES), lambda k, b: (0, k))],
        out_specs=[tok, pl.BlockSpec((None, gs, 2 * SSM_STATE), lambda k, b: (b, k, 0))],
        out_shape=[jax.ShapeDtypeStruct(u.shape, F32),
                   jax.ShapeDtypeStruct((bsz, n_g, 2 * SSM_STATE), F32)],
        scratch_shapes=[pltpu.VMEM((n_c, wd), BF16),
                        pltpu.VMEM((n_c * gs, 2 * SSM_STATE), F32),
                        pltpu.VMEM((n_c * gs, 2 * SSM_STATE), F32),
                        pltpu.VMEM((n_c, sd), BF16)],
        compiler_params=_params(("parallel", "parallel")),
        name="s5_prompt",
    )(u, s_bd, a_bd, e_bd, lam_a, lam_b, d)


def _s5_step(u, tables, h0, gs):
    s_t, a_t, e_t, lam_a, lam_b, d_t = tables
    bsz, seq, chunk = u.shape[0], 1, 1
    n_g = u.shape[1] // SSM_GROUP
    n_sg = n_g // gs
    n_c = 1
    rows = bsz
    w = gs * SSM_GROUP
    sw = gs * 2 * SSM_STATE
    up = u

    s_loc = pl.pallas_call(
        _s5_local_body,
        grid=(n_sg,),
        in_specs=[pl.BlockSpec((rows, w), lambda s: (0, s)),
                  pl.BlockSpec((None, w, sw), lambda s: (s, 0, 0))],
        out_specs=pl.BlockSpec((rows, sw), lambda s: (0, s)),
        out_shape=jax.ShapeDtypeStruct((rows, n_g * 2 * SSM_STATE), F32),
        compiler_params=_params(("parallel",)),
        name="s5_local",
    )(up, s_t)

    r2 = bsz * n_g
    rb = min(r2, 1024 if n_c == 1 else 64)
    lane = 2 * SSM_STATE
    la = jnp.broadcast_to(lam_a[None], (bsz, n_g, lane)).reshape(r2, lane)
    lb = jnp.broadcast_to(lam_b[None], (bsz, n_g, lane)).reshape(r2, lane)
    hc, h_last = pl.pallas_call(
        functools.partial(_s5_scan_body, n_chunks=n_c),
        grid=(r2 // rb,),
        in_specs=[pl.BlockSpec((n_c, rb, lane), lambda r: (0, r, 0)),
                  pl.BlockSpec((rb, lane), lambda r: (r, 0)),
                  pl.BlockSpec((rb, lane), lambda r: (r, 0)),
                  pl.BlockSpec((rb, lane), lambda r: (r, 0))],
        out_specs=[pl.BlockSpec((n_c, rb, lane), lambda r: (0, r, 0)),
                   pl.BlockSpec((rb, lane), lambda r: (r, 0))],
        out_shape=[jax.ShapeDtypeStruct((n_c, r2, lane), F32),
                   jax.ShapeDtypeStruct((r2, lane), F32)],
        compiler_params=_params(("parallel",)),
        name="s5_scan",
    )(s_loc.reshape(n_c, r2, lane), h0.reshape(r2, lane), la, lb)

    yp = pl.pallas_call(
        _s5_out_body,
        grid=(n_sg,),
        in_specs=[pl.BlockSpec((rows, w), lambda s: (0, s)),
                  pl.BlockSpec((None, w, w), lambda s: (s, 0, 0)),
                  pl.BlockSpec((rows, sw), lambda s: (0, s)),
                  pl.BlockSpec((None, sw, w), lambda s: (s, 0, 0)),
                  pl.BlockSpec((1, w), lambda s: (0, s))],
        out_specs=pl.BlockSpec((rows, w), lambda s: (0, s)),
        out_shape=jax.ShapeDtypeStruct((rows, n_g * chunk * SSM_GROUP), F32),
        compiler_params=_params(("parallel",)),
        name="s5_out",
    )(up, a_t, hc.reshape(rows, n_g * lane), e_t, d_t)
    return yp, h_last.reshape(bsz, n_g, lane)


def _cross_prompt_body(q_ref, k_ref, v_ref, o_ref, *, n_heads):
    scale = CROSS_HEAD_DIM ** -0.5
    for h in range(n_heads):
        sl = slice(h * CROSS_HEAD_DIM, (h + 1) * CROSS_HEAD_DIM)
        s = _nt_dot(q_ref[:, sl], k_ref[:, sl].astype(BF16)) * scale
        e = jnp.exp(s - jnp.max(s, axis=-1, keepdims=True))
        l = jnp.sum(e, axis=-1, keepdims=True)
        o = jnp.dot(e.astype(BF16), v_ref[:, sl].astype(BF16), preferred_element_type=F32)
        o_ref[:, sl] = (o / l).astype(o_ref.dtype)


def _cross_prompt(q, mem_k, mem_v, bsz, seq):
    dc = q.shape[1]
    n_mem = mem_k.shape[0] // bsz
    tq = min(seq, 1024)
    nq = seq // tq
    return pl.pallas_call(
        functools.partial(_cross_prompt_body, n_heads=dc // CROSS_HEAD_DIM),
        grid=(bsz, nq),
        in_specs=[pl.BlockSpec((tq, dc), lambda b, i: (b * nq + i, 0)),
                  pl.BlockSpec((n_mem, dc), lambda b, i: (b, 0)),
                  pl.BlockSpec((n_mem, dc), lambda b, i: (b, 0))],
        out_specs=pl.BlockSpec((tq, dc), lambda b, i: (b * nq + i, 0)),
        out_shape=jax.ShapeDtypeStruct((bsz * seq, dc), BF16),
        compiler_params=_params(("parallel", "parallel")),
        name="cross_prompt",
    )(q, mem_k, mem_v)


def _cross_sample_body(q_ref, k_ref, v_ref, ones_ref, o_ref):
    bs, rows, dh = k_ref.shape
    n_heads = q_ref.shape[1]
    scale = CROSS_HEAD_DIM ** -0.5
    q = q_ref[...]
    row_head = lax.broadcasted_iota(I32, (bs, rows, dh), 1) % n_heads
    qt = jnp.broadcast_to(q[:, 0:1, :], (bs, rows, dh))
    for h in range(1, n_heads):
        qt = jnp.where(row_head == h, q[:, h:h + 1, :], qt)
    kq = (k_ref[...] * qt).reshape(bs * rows, dh).astype(BF16)
    s = jnp.dot(kq, ones_ref[...], preferred_element_type=F32) * scale
    s = s.reshape(bs, rows // SUBLANES, SUBLANES, dh)

    def over_memory(x, reduce, combine):
        r = reduce(x, axis=1)
        shift = n_heads
        while shift < SUBLANES:
            r = combine(r, pltpu.roll(r, shift, 1))
            shift *= 2
        return r

    m = over_memory(s, jnp.max, jnp.maximum)
    e = jnp.exp(s - m[:, None])
    p = e / over_memory(e, jnp.sum, jnp.add)[:, None]
    o_ref[...] = over_memory(p * v_ref[...].reshape(s.shape), jnp.sum, jnp.add)


def _cross_sample(q, cache_k, cache_v, layer):
    bsz, dc = q.shape
    n_heads = dc // CROSS_HEAD_DIM
    rows = cache_k.shape[2]
    bs = SUBLANES
    kv_spec = pl.BlockSpec((None, bs, rows, CROSS_HEAD_DIM), lambda i: (layer, i, 0, 0))
    out = pl.pallas_call(
        _cross_sample_body,
        grid=(bsz // bs,),
        in_specs=[pl.BlockSpec((bs, n_heads, CROSS_HEAD_DIM), lambda i: (i, 0, 0)), kv_spec, kv_spec,
                  pl.BlockSpec((CROSS_HEAD_DIM, LANES), lambda i: (0, 0))],
        out_specs=pl.BlockSpec((bs, SUBLANES, CROSS_HEAD_DIM), lambda i: (i, 0, 0)),
        out_shape=jax.ShapeDtypeStruct((bsz, SUBLANES, CROSS_HEAD_DIM), F32),
        compiler_params=_params(("parallel",)),
        name="cross_sample",
    )(q.reshape(bsz, n_heads, CROSS_HEAD_DIM), cache_k, cache_v, jnp.ones((CROSS_HEAD_DIM, LANES), BF16))
    return out[:, :n_heads].reshape(bsz, dc).astype(BF16)


def _softmax_step(s, v, m, l, acc):
    m_new = jnp.maximum(m, jnp.max(s, axis=-1, keepdims=True))
    corr = jnp.exp(m - m_new)
    p = jnp.exp(s - m_new)
    l = l * corr + jnp.sum(p, axis=-1, keepdims=True)
    acc = acc * corr + jnp.dot(p.astype(BF16), v, preferred_element_type=F32)
    return m_new, l, acc


def _diff_prompt_body(lam_ref, q_ref, k_ref, v_ref, g_ref, o_ref, *, tq, lambda_init):
    qi = pl.program_id(2)
    q = q_ref[...] * DIFF_HEAD_DIM ** -0.5
    lane = lax.broadcasted_iota(I32, q.shape, 1)
    zero = jnp.zeros_like(q)
    q1 = jnp.where(lane < DIFF_HEAD_DIM, q, zero)
    q2 = jnp.where(lane >= DIFF_HEAD_DIM, q, zero)

    def chunk(kc, carry, masked):
        off = pl.multiple_of(kc * tq, tq)
        k = k_ref[pl.ds(off, tq), :]
        v = v_ref[pl.ds(off, tq), :]
        s1 = _nt_dot(q1, k)
        s2 = _nt_dot(q2, k)
        if masked:
            keep = (lax.broadcasted_iota(I32, s1.shape, 1) <= lax.broadcasted_iota(I32, s1.shape, 0))
            s1 = jnp.where(keep, s1, -jnp.inf)
            s2 = jnp.where(keep, s2, -jnp.inf)
        m1, l1, a1, m2, l2, a2 = carry
        m1, l1, a1 = _softmax_step(s1, v, m1, l1, a1)
        m2, l2, a2 = _softmax_step(s2, v, m2, l2, a2)
        return m1, l1, a1, m2, l2, a2

    col = lambda val: jnp.full((tq, 1), val, F32)
    acc0 = jnp.zeros((tq, DIFF_V_DIM), F32)
    carry = (col(-jnp.inf), col(0.0), acc0, col(-jnp.inf), col(0.0), acc0)
    carry = lax.fori_loop(0, qi, lambda kc, cr: chunk(kc, cr, False), carry)
    m1, l1, a1, m2, l2, a2 = chunk(qi, carry, True)
    w = a1 / l1 - lam_ref[0] * (a2 / l2)
    o_ref[...] = (_rms(w, g_ref[...]) * (1.0 - lambda_init)).astype(o_ref.dtype)


def _diff_prompt(q, k, v, lam, subln, bsz, seq, lambda_init):
    n_heads = q.shape[1] // DIFF_V_DIM
    tq = min(seq, 512)
    nq = seq // tq
    kv_spec = pl.BlockSpec((seq, DIFF_V_DIM), lambda b, h, i: (b, h))
    q_spec = pl.BlockSpec((tq, DIFF_V_DIM), lambda b, h, i: (b * nq + i, h))
    return pl.pallas_call(
        functools.partial(_diff_prompt_body, tq=tq, lambda_init=lambda_init),
        grid=(bsz, n_heads, nq),
        in_specs=[pl.BlockSpec(memory_space=pltpu.SMEM), q_spec, kv_spec, kv_spec,
                  pl.BlockSpec((1, DIFF_V_DIM), lambda b, h, i: (0, 0))],
        out_specs=q_spec,
        out_shape=jax.ShapeDtypeStruct(q.shape, BF16),
        compiler_params=_params(("parallel", "parallel", "arbitrary")),
        name="diff_prompt",
    )(lam, q, k, v, subln)


def _diff_sample_body(pt_ref, lam_ref, q_ref, kn_ref, vn_ref, g_ref, x_ref, *rest, n_heads, lambda_init):
    k_refs = rest[:PAGES_PER_STEP]
    v_refs = rest[PAGES_PER_STEP:2 * PAGES_PER_STEP]
    o_ref, qb_ref, m_ref, l_ref, acc_ref = rest[2 * PAGES_PER_STEP:]
    step = pl.program_id(1)
    n_rows = 2 * n_heads
    width = n_heads * DIFF_V_DIM
    row = lax.broadcasted_iota(I32, (n_rows, width), 0)
    col = lax.broadcasted_iota(I32, (n_rows, width), 1)
    own = (col // DIFF_V_DIM == row % n_heads) & ((col // DIFF_HEAD_DIM) % 2 == row // n_heads)

    @pl.when(step == 0)
    def _():
        qs = q_ref[...] * DIFF_HEAD_DIM ** -0.5
        qb_ref[...] = jnp.where(own, jnp.broadcast_to(qs, (n_rows, width)), 0.0).astype(BF16)
        m_ref[...] = jnp.full(m_ref.shape, -jnp.inf, F32)
        l_ref[...] = jnp.zeros(l_ref.shape, F32)
        acc_ref[...] = jnp.zeros(acc_ref.shape, F32)

    qb = qb_ref[...]
    m, l, acc = m_ref[...], l_ref[...], acc_ref[...]
    pe_shape = (n_rows, x_ref.shape[1])
    mine = (lax.broadcasted_iota(I32, pe_shape, 1) % n_heads
            == lax.broadcasted_iota(I32, pe_shape, 0) % n_heads)
    for k_ref, v_ref in zip(k_refs, v_refs):
        s = jnp.dot(qb, k_ref[...].astype(BF16), preferred_element_type=F32)
        m_new = jnp.maximum(m, jnp.max(s, axis=-1, keepdims=True))
        corr = jnp.exp(m - m_new)
        p = jnp.exp(s - m_new)
        l = l * corr + jnp.sum(p, axis=-1, keepdims=True)
        pe = jnp.dot(p.astype(BF16), x_ref[...], preferred_element_type=F32)
        pe = jnp.where(mine, pe, 0.0).astype(BF16)
        acc = acc * corr + jnp.dot(pe, v_ref[...].astype(BF16), preferred_element_type=F32)
        m = m_new
    m_ref[...], l_ref[...], acc_ref[...] = m, l, acc

    @pl.when(step == pl.num_programs(1) - 1)
    def _():
        s_new = jnp.sum(qb.astype(F32) * kn_ref[...], axis=-1, keepdims=True)
        m_new = jnp.maximum(m, s_new)
        corr = jnp.exp(m - m_new)
        p = jnp.exp(s_new - m_new)
        l_fin = l * corr + p
        vn = vn_ref[...]
        o = (acc * corr + p * jnp.concatenate([vn, vn], axis=0)) / l_fin
        w = o[:n_heads] - lam_ref[0] * o[n_heads:]
        o_ref[...] = (_rms(w, g_ref[...]) * (1.0 - lambda_init)).astype(o_ref.dtype)


def _diff_sample(q, k_new, v_new, cache_kt, cache_v, layer, page_table, lam, subln, lambda_init):
    bsz, width = q.shape
    n_heads = width // DIFF_V_DIM
    page = cache_kt.shape[3]
    n_pages = page_table.shape[1]
    n_steps = n_pages // PAGES_PER_STEP
    assert n_pages % PAGES_PER_STEP == 0
    row_spec = pl.BlockSpec((None, 1, width), lambda b, s, pt: (b, 0, 0))
    head_spec = pl.BlockSpec((None, n_heads, DIFF_V_DIM), lambda b, s, pt: (b, 0, 0))

    def page_spec(k, rows, cols):
        return pl.BlockSpec((None, None, rows, cols),
                            lambda b, s, pt, k=k: (layer, pt[b * n_pages + s * PAGES_PER_STEP + k], 0, 0))

    expand = (lax.broadcasted_iota(I32, (page, page * n_heads), 1) // n_heads
              == lax.broadcasted_iota(I32, (page, page * n_heads), 0)).astype(BF16)
    in_specs = [pl.BlockSpec(memory_space=pltpu.SMEM), row_spec, row_spec, head_spec,
                pl.BlockSpec((1, DIFF_V_DIM), lambda b, s, pt: (0, 0)),
                pl.BlockSpec((page, page * n_heads), lambda b, s, pt: (0, 0))]
    in_specs += [page_spec(k, width, page) for k in range(PAGES_PER_STEP)]
    in_specs += [page_spec(k, page * n_heads, DIFF_V_DIM) for k in range(PAGES_PER_STEP)]
    grid_spec = pltpu.PrefetchScalarGridSpec(
        num_scalar_prefetch=1,
        grid=(bsz, n_steps),
        in_specs=in_specs,
        out_specs=head_spec,
        scratch_shapes=[pltpu.VMEM((2 * n_heads, width), BF16),
                        pltpu.VMEM((2 * n_heads, 1), F32),
                        pltpu.VMEM((2 * n_heads, 1), F32),
                        pltpu.VMEM((2 * n_heads, DIFF_V_DIM), F32)],
    )
    out = pl.pallas_call(
        functools.partial(_diff_sample_body, n_heads=n_heads, lambda_init=lambda_init),
        grid_spec=grid_spec,
        out_shape=jax.ShapeDtypeStruct((bsz, n_heads, DIFF_V_DIM), BF16),
        compiler_params=_params(("parallel", "arbitrary")),
        name="diff_sample",
    )(page_table.reshape(-1), lam, q.reshape(bsz, 1, width), k_new.reshape(bsz, 1, width),
      v_new.reshape(bsz, n_heads, DIFF_V_DIM), subln, expand,
      *([cache_kt] * PAGES_PER_STEP), *([cache_v] * PAGES_PER_STEP))
    return out.reshape(bsz, width)


def _split_bf16(v):
    hi = v.astype(BF16)
    return hi, (v - hi.astype(F32)).astype(BF16)


def _to_slabs(slab_ref, value):
    t, width = value.shape
    n_slab = width // LANES
    for s in range(n_slab):
        slab_ref[pl.ds(s, t, stride=n_slab), :] = value[:, s * LANES:(s + 1) * LANES].astype(slab_ref.dtype)


def _slab_column(slab_ref, s, t, n_slab, first_row=0):
    return slab_ref[pl.ds(first_row + s, t, stride=n_slab), :]


def _start_slab_copies(src_ref, dst_ref, sem, index_of, n_copies, n_slab, unroll=8):
    def body(it, carry):
        for u in range(unroll):
            r = it * unroll + u
            src = pl.multiple_of(index_of(r) * n_slab, n_slab)
            dst = pl.multiple_of(r * n_slab, n_slab)
            pltpu.make_async_copy(src_ref.at[pl.ds(src, n_slab)], dst_ref.at[pl.ds(dst, n_slab)], sem).start()
        return carry

    lax.fori_loop(0, n_copies // unroll, body, 0)


def _wait_slab_copies(src_ref, dst_ref, sem):
    pltpu.make_async_copy(src_ref.at[pl.ds(0, dst_ref.shape[0])], dst_ref, sem).wait()


def _router_body(xp_ref, xs_ref, g_ref, w_ref, b_ref, hf_ref, eid_ref, gate_ref, *, n_prompt_tiles):
    i = pl.program_id(0)
    x = jnp.where(i < n_prompt_tiles, xp_ref[...], xs_ref[...])
    hf = _rms(x, g_ref[...])
    _to_slabs(hf_ref, hf)
    h_hi, h_lo = _split_bf16(hf)
    w_hi, w_lo = _split_bf16(w_ref[...])
    dot = lambda a, b: jnp.dot(a, b, preferred_element_type=F32)
    logits = dot(h_hi, w_hi) + (dot(h_lo, w_hi) + dot(h_hi, w_lo)) + b_ref[...]
    lane = lax.broadcasted_iota(I32, logits.shape, 1)
    lane_f = lane.astype(F32)
    first = lambda hit: jnp.min(jnp.where(hit, lane_f, float(LANES)), axis=-1, keepdims=True).astype(I32)
    gl = jnp.where(lane < N_GROUPS, logits, -jnp.inf)
    g_max = jnp.max(gl, axis=-1, keepdims=True)
    g_idx = first(gl == g_max)
    g_w = 1.0 / jnp.sum(jnp.exp(gl - g_max), axis=-1, keepdims=True)
    lo = N_GROUPS + EXPERTS_PER_GROUP * g_idx
    in_group = (lane >= lo) & (lane < lo + EXPERTS_PER_GROUP)
    el = jnp.where(in_group, logits, -jnp.inf)
    ee = jnp.exp(el - jnp.max(el, axis=-1, keepdims=True))
    prob = jnp.where(in_group, ee / jnp.sum(ee, axis=-1, keepdims=True), -1.0)
    p1 = jnp.max(prob, axis=-1, keepdims=True)
    i1 = first(prob == p1)
    rest = jnp.where(lane == i1, -1.0, prob)
    p2 = jnp.max(rest, axis=-1, keepdims=True)
    i2 = first(rest == p2)
    denom = p1 + p2
    eid_ref[...] = jnp.where(lane == 0, i1 - N_GROUPS, jnp.where(lane == 1, i2 - N_GROUPS, 0))
    gate_ref[...] = jnp.where(lane == 0, g_w * p1 / denom, jnp.where(lane == 1, g_w * p2 / denom, 0.0))


def _router(xp, xs, g, w_cat, b_cat):
    d = xp.shape[1]
    t = MOE_TOK_TILE
    n_p, n_s = xp.shape[0] // t, xs.shape[0] // t
    n = xp.shape[0] + xs.shape[0]
    tile = lambda cols: pl.BlockSpec((t, cols), lambda i: (i, 0))
    return pl.pallas_call(
        functools.partial(_router_body, n_prompt_tiles=n_p),
        grid=(n_p + n_s,),
        in_specs=[pl.BlockSpec((t, d), lambda i: (jnp.minimum(i, n_p - 1), 0)),
                  pl.BlockSpec((t, d), lambda i: (jnp.maximum(i - n_p, 0), 0)),
                  pl.BlockSpec((1, d), lambda i: (0, 0)),
                  pl.BlockSpec((d, LANES), lambda i: (0, 0)),
                  pl.BlockSpec((1, LANES), lambda i: (0, 0))],
        out_specs=[pl.BlockSpec((t * (d // LANES), LANES), lambda i: (i, 0)), tile(LANES), tile(LANES)],
        out_shape=[jax.ShapeDtypeStruct((n * (d // LANES), LANES), F32),
                   jax.ShapeDtypeStruct((n, LANES), I32),
                   jax.ShapeDtypeStruct((n, LANES), F32)],
        compiler_params=_params(("parallel",)),
        name="moe_router",
    )(xp, xs, g, w_cat, b_cat)


EXPERT_UP_CHUNK = 256
EXPERT_DOWN_CHUNK = 1024


def _experts_body(be_ref, nblk_ref, tok_ref, src_ref, wg_ref, wu_ref, wd_ref, o_ref,
                  xin_ref, xb_ref, hb_ref, sems, *, n_up, n_down):
    blk = pl.program_id(0)
    c = pl.program_id(1)
    rows, d = xb_ref.shape
    n_slab = d // LANES
    n_used = nblk_ref[0]
    buf = blk % 2

    def fetch(b, into):
        _start_slab_copies(src_ref, xin_ref.at[into], sems.at[into],
                           lambda r: tok_ref[b * rows + r], rows, n_slab)

    @pl.when(blk < n_used)
    def _():
        @pl.when(c == 0)
        def _():
            @pl.when(blk == 0)
            def _():
                fetch(0, 0)

            _wait_slab_copies(src_ref, xin_ref.at[buf], sems.at[buf])
            for s in range(n_slab):
                xb_ref[:, s * LANES:(s + 1) * LANES] = _slab_column(xin_ref.at[buf], s, rows, n_slab).astype(BF16)

        @pl.when((c == 1) & (blk + 1 < n_used))
        def _():
            fetch(blk + 1, 1 - buf)

        @pl.when(c < n_up)
        def _():
            xb = xb_ref[...]
            gate = jnp.dot(xb, wg_ref[...].astype(BF16), preferred_element_type=F32)
            up = jnp.dot(xb, wu_ref[...].astype(BF16), preferred_element_type=F32)
            hb_ref[c] = (jax.nn.silu(gate) * up).astype(BF16)

        ck = hb_ref.shape[2]
        per_step = wd_ref.shape[1] // LANES
        for j in range(n_down):
            @pl.when(c == n_up + j)
            def _():
                wd = wd_ref[...].astype(BF16)
                y = None
                for k in range(n_up):
                    part = jnp.dot(hb_ref[k], wd[k * ck:(k + 1) * ck, :], preferred_element_type=F32)
                    y = part if y is None else y + part
                for s in range(per_step):
                    o_ref[pl.ds(j * per_step + s, rows, stride=n_slab), :] = y[:, s * LANES:(s + 1) * LANES]

    @pl.when((blk >= n_used) & (c == 0))
    def _():
        o_ref[...] = jnp.zeros(o_ref.shape, o_ref.dtype)


def _experts(src, slot_tok, block_e, n_used, w_gate, w_up, w_down, layer):
    d, de = w_gate.shape[2], w_gate.shape[3]
    n_slab = d // LANES
    n_blocks = slot_tok.shape[0] // MOE_ROWS
    n_up = de // EXPERT_UP_CHUNK
    n_down = d // EXPERT_DOWN_CHUNK
    assert n_up >= 2

    def expert(b, be, nb):
        return be[jnp.minimum(b, nb[0] - 1)]

    up_spec = pl.BlockSpec((None, None, d, EXPERT_UP_CHUNK),
                           lambda b, c, be, nb, tok: (layer, expert(b, be, nb), 0, jnp.minimum(c, n_up - 1)))
    grid_spec = pltpu.PrefetchScalarGridSpec(
        num_scalar_prefetch=3,
        grid=(n_blocks, n_up + n_down),
        in_specs=[pl.BlockSpec(memory_space=pl.ANY), up_spec, up_spec,
                  pl.BlockSpec((None, None, de, EXPERT_DOWN_CHUNK),
                               lambda b, c, be, nb, tok: (layer, expert(b, be, nb), 0, jnp.maximum(c - n_up, 0)))],
        out_specs=pl.BlockSpec((MOE_ROWS * n_slab, LANES), lambda b, c, be, nb, tok: (b, 0)),
        scratch_shapes=[pltpu.VMEM((2, MOE_ROWS * n_slab, LANES), F32),
                        pltpu.VMEM((MOE_ROWS, d), BF16),
                        pltpu.VMEM((n_up, MOE_ROWS, EXPERT_UP_CHUNK), BF16),
                        pltpu.SemaphoreType.DMA((2,))],
    )
    return pl.pallas_call(
        functools.partial(_experts_body, n_up=n_up, n_down=n_down),
        grid_spec=grid_spec,
        out_shape=jax.ShapeDtypeStruct((n_blocks * MOE_ROWS * n_slab, LANES), F32),
        compiler_params=_params(("arbitrary", "arbitrary")),
        name="moe_experts",
    )(block_e, n_used, slot_tok, src, w_gate, w_up, w_down)


def _combine_body(dest_ref, xp_ref, xs_ref, gate_ref, y_ref, g_ref, *rest, n_prompt_tiles, final_norm):
    if final_norm:
        op_ref, os_ref, buf_ref, sem = rest
    else:
        o_ref, buf_ref, sem = rest
    i = pl.program_id(0)
    t, d = xp_ref.shape
    n_slab = d // LANES
    cur = i % 2

    def fetch(tile, into):
        _start_slab_copies(y_ref, buf_ref.at[into], sem.at[into],
                           lambda a: dest_ref[tile * 2 * t + a], 2 * t, n_slab)

    @pl.when(i == 0)
    def _():
        fetch(0, 0)

    @pl.when(i + 1 < pl.num_programs(0))
    def _():
        fetch(i + 1, 1 - cur)

    buf = buf_ref.at[cur]
    _wait_slab_copies(y_ref, buf, sem.at[cur])
    x = jnp.where(i < n_prompt_tiles, xp_ref[...], xs_ref[...])
    gates = gate_ref[...]
    g0, g1 = gates[:, 0:1], gates[:, 1:2]
    pieces = []
    for s in range(n_slab):
        y0 = _slab_column(buf, s, t, 2 * n_slab)
        y1 = _slab_column(buf, s, t, 2 * n_slab, first_row=n_slab)
        pieces.append(x[:, s * LANES:(s + 1) * LANES] + (y0 * g0 + y1 * g1))
    out = jnp.concatenate(pieces, axis=1)
    if final_norm:
        out = _rms(out, g_ref[...])

        @pl.when(i < n_prompt_tiles)
        def _():
            op_ref[...] = out

        @pl.when(i >= n_prompt_tiles)
        def _():
            os_ref[...] = out
    else:
        o_ref[...] = out


def _combine(xp, xs, gates, y_sorted, dest, g_final, final_norm):
    d = xp.shape[1]
    t = MOE_TOK_TILE
    n_p, n_s = xp.shape[0] // t, xs.shape[0] // t
    n = xp.shape[0] + xs.shape[0]
    p_map = lambda i, dst: (jnp.minimum(i, n_p - 1), 0)
    s_map = lambda i, dst: (jnp.maximum(i - n_p, 0), 0)
    if final_norm:
        out_specs = [pl.BlockSpec((t, d), p_map), pl.BlockSpec((t, d), s_map)]
        out_shape = [jax.ShapeDtypeStruct(xp.shape, F32), jax.ShapeDtypeStruct(xs.shape, F32)]
    else:
        out_specs = pl.BlockSpec((t, d), lambda i, dst: (i, 0))
        out_shape = jax.ShapeDtypeStruct((n, d), F32)
    grid_spec = pltpu.PrefetchScalarGridSpec(
        num_scalar_prefetch=1,
        grid=(n_p + n_s,),
        in_specs=[pl.BlockSpec((t, d), p_map), pl.BlockSpec((t, d), s_map),
                  pl.BlockSpec((t, LANES), lambda i, dst: (i, 0)),
                  pl.BlockSpec(memory_space=pl.ANY),
                  pl.BlockSpec((1, d), lambda i, dst: (0, 0))],
        out_specs=out_specs,
        scratch_shapes=[pltpu.VMEM((2, 2 * t * (d // LANES), LANES), F32), pltpu.SemaphoreType.DMA((2,))],
    )
    return pl.pallas_call(
        functools.partial(_combine_body, n_prompt_tiles=n_p, final_norm=final_norm),
        grid_spec=grid_spec,
        out_shape=out_shape,
        compiler_params=_params(("arbitrary",)),
        name="moe_combine",
    )(dest, xp, xs, gates, y_sorted, g_final)


def _moe(xp, xs, layer, g_ffn, w_group, b_group, w_router, b_router, w_gate, w_up, w_down, g_final, final_norm):
    d = xp.shape[1]
    n_tok = xp.shape[0] + xs.shape[0]
    pad = LANES - N_GROUPS - N_EXPERTS
    w_cat = jnp.concatenate([w_group[layer], w_router[layer], jnp.zeros((d, pad), F32)], axis=1)
    b_cat = jnp.concatenate([b_group[layer], b_router[layer], jnp.zeros((pad,), F32)])[None]
    hf, eid, gates = _router(xp, xs, g_ffn, w_cat, b_cat)

    e_flat = eid[:, :2].reshape(-1)
    n_asg = e_flat.shape[0]
    onehot = (e_flat[:, None] == jnp.arange(N_EXPERTS, dtype=I32)[None, :]).astype(I32)
    csum = jnp.cumsum(onehot, axis=0)
    rank = jnp.sum((csum - onehot) * onehot, axis=1)
    counts = csum[-1]
    padded = ((counts + MOE_ROWS - 1) // MOE_ROWS) * MOE_ROWS
    pends = jnp.cumsum(padded)
    pstarts = pends - padded
    dest = (jnp.sum(onehot * pstarts[None, :], axis=1) + rank).astype(I32)
    n_blocks = -(-n_asg // MOE_ROWS) + N_EXPERTS
    n_slots = n_blocks * MOE_ROWS
    slot_tok = jnp.zeros((n_slots,), I32).at[dest].set(jnp.arange(n_asg, dtype=I32) // 2)
    block_start = jnp.arange(n_blocks, dtype=I32) * MOE_ROWS
    block_e = jnp.minimum(jnp.sum((block_start[:, None] >= pends[None, :]).astype(I32), axis=1),
                          N_EXPERTS - 1).astype(I32)
    n_used = (pends[-1] // MOE_ROWS).astype(I32).reshape(1)

    y_sorted = _experts(hf, slot_tok, block_e, n_used, w_gate, w_up, w_down, layer)
    return _combine(xp, xs, gates, y_sorted, dest, g_final, final_norm)


def kernel(x_prompt, x_sample, state_conv, state_s5_re, state_s5_im, cache_k, cache_v, cache_mem_k, cache_mem_v, page_table, mem_prompt, norm_mix, norm_cross, norm_ffn, norm_final, w_in_rec, conv_w, ssm_lambda_re, ssm_lambda_im, ssm_log_dt, ssm_b_re, ssm_b_im, ssm_c_re, ssm_c_im, ssm_d, ssm_w_glu, w_out_rec, w_qkv, diff_lambda_q1, diff_lambda_k1, diff_lambda_q2, diff_lambda_k2, diff_subln, w_o_attn, w_cross_q, w_cross_k, w_cross_v, w_cross_o, w_group, b_group, w_router, b_router, w_exp_gate, w_exp_up, w_exp_down):
    bp, seq, d = x_prompt.shape
    bs = x_sample.shape[0]
    depth = norm_mix.shape[0]
    n_p = bp * seq
    d_conv = conv_w.shape[2]
    n_g = ssm_lambda_re.shape[1]
    n_mem = mem_prompt.shape[1]
    d_cross = w_cross_q.shape[2]
    d_attn = w_o_attn.shape[1] if w_o_attn.shape[0] else d
    lane2 = 2 * SSM_STATE

    xc = None
    xp = x_prompt.reshape(n_p, d)
    xs = x_sample.reshape(bs, d)
    mem = mem_prompt.reshape(bp * n_mem, d)

    conv_p, conv_s, s5_p, s5_s = [], [], [], []
    k_p, v_p, k_s, v_s, memk_p, memv_p = [], [], [], [], [], []

    for layer in range(depth):
        g_mix = norm_mix[layer][None]
        src_p, off_p = (xp, 0) if xc is None else (xc, 0)
        src_s, off_s = (xs, 0) if xc is None else (xc, n_p)
        if layer % 2 == 0:
            r = layer // 2
            yc_p, u_p, tails = _rec_in(src_p, off_p, n_p, seq, g_mix, w_in_rec, r, conv_w[r])
            yc_s, u_s, uc_s = _rec_in(src_s, off_s, bs, 1, g_mix, w_in_rec, r, conv_w[r], buf=state_conv[r])
            tps = tails.shape[0] // bp
            conv_p.append(tails[tps - 1::tps])
            conv_s.append(jnp.stack([state_conv[r][:, 1], uc_s], axis=1))
            ssm = (ssm_lambda_re[r], ssm_lambda_im[r], ssm_log_dt[r], ssm_b_re[r], ssm_b_im[r],
                   ssm_c_re[r], ssm_c_im[r], ssm_d[r])
            y_p, h_p = _s5_prompt(u_p, bp, seq, _s5_tables(*ssm, S5_CHUNK, LANES // SSM_GROUP), S5_CHUNK)
            h0_s = jnp.concatenate([state_s5_re[r], state_s5_im[r]], axis=-1).reshape(bs, n_g * lane2)
            y_s, h_s = _s5_step(u_s, _s5_tables(*ssm, 1, SUBLANES), h0_s, SUBLANES)
            s5_p.append(h_p)
            s5_s.append(h_s)
            new = []
            for yc, y, m, src, off in ((yc_p, y_p, n_p, src_p, off_p), (yc_s, y_s, bs, src_s, off_s)):
                y_ssm = _mm([y], [ssm_w_glu], m=m, n=d_conv, layer=r, prologue="gelu", epilogue="glu", e=y,
                            out_dtypes=(BF16,), name="s5_glu")
                new.append(_mm([yc, y_ssm], [w_out_rec, w_out_rec], m=m, n=d, layer=r, w_rows=(0, 1),
                               epilogue="res", e=src, e_off=off, name="rec_out"))
            xp, xs = new
        else:
            a = layer // 2
            lambda_init = 0.8 - 0.6 * math.exp(-0.3 * layer)
            lam = (jnp.exp(jnp.sum(diff_lambda_q1[a] * diff_lambda_k1[a]))
                   - jnp.exp(jnp.sum(diff_lambda_q2[a] * diff_lambda_k2[a])) + lambda_init).reshape(1)
            subln = diff_subln[a][None]
            qkv = []
            for m, src, off in ((n_p, src_p, off_p), (bs, src_s, off_s)):
                sample = m == bs
                proj = functools.partial(_mm, [src], [w_qkv], m=m, n=d_attn, layer=a, x_off=off,
                                         prologue="norm", g=g_mix)
                q = proj(col_off=0, out_dtypes=(F32,) if sample else (BF16,), name="attn_q")
                if sample:
                    k = proj(col_off=d_attn, name="attn_k")
                    v = proj(col_off=2 * d_attn, name="attn_v")
                    qkv.append((q, k, v))
                else:
                    k, kb = proj(col_off=d_attn, out_dtypes=(F32, BF16), name="attn_k")
                    v, vb = proj(col_off=2 * d_attn, out_dtypes=(F32, BF16), name="attn_v")
                    qkv.append((q, k, v, kb, vb))
            q, k, v, kb, vb = qkv[0]
            k_p.append(k)
            v_p.append(v)
            o_p = _diff_prompt(q, kb, vb, lam, subln, bp, seq, lambda_init)
            q, k, v = qkv[1]
            k_s.append(k)
            v_s.append(v)
            n_pool, page = cache_k.shape[1], cache_k.shape[2]
            cache_kt = jnp.transpose(cache_k, (0, 1, 3, 4, 5, 2)).reshape(-1, n_pool, d_attn, page)
            cache_vr = cache_v.reshape(-1, n_pool, page * (d_attn // DIFF_V_DIM), DIFF_V_DIM)
            o_s = _diff_sample(q, k, v, cache_kt, cache_vr, a, page_table, lam, subln, lambda_init)
            xp = _mm([o_p], [w_o_attn], m=n_p, n=d, layer=a, epilogue="res", e=src_p, e_off=off_p, name="attn_out")
            xs = _mm([o_s], [w_o_attn], m=bs, n=d, layer=a, epilogue="res", e=src_s, e_off=off_s, name="attn_out")

        g_cross = norm_cross[layer][None]
        mk = _mm([mem], [w_cross_k], m=bp * n_mem, n=d_cross, layer=layer, name="mem_k")
        mv = _mm([mem], [w_cross_v], m=bp * n_mem, n=d_cross, layer=layer, name="mem_v")
        memk_p.append(mk)
        memv_p.append(mv)
        q_p = _mm([xp], [w_cross_q], m=n_p, n=d_cross, layer=layer, prologue="norm", g=g_cross,
                  out_dtypes=(BF16,), name="cross_q")
        o_p = _cross_prompt(q_p, mk, mv, bp, seq)
        xp = _mm([o_p], [w_cross_o], m=n_p, n=d, layer=layer, epilogue="res", e=xp, name="cross_out")
        q_s = _mm([xs], [w_cross_q], m=bs, n=d_cross, layer=layer, prologue="norm", g=g_cross, name="cross_q")
        mem_rows = n_mem * (d_cross // CROSS_HEAD_DIM)
        o_s = _cross_sample(q_s, cache_mem_k.reshape(depth, bs, mem_rows, CROSS_HEAD_DIM),
                            cache_mem_v.reshape(depth, bs, mem_rows, CROSS_HEAD_DIM), layer)
        xs = _mm([o_s], [w_cross_o], m=bs, n=d, layer=layer, epilogue="res", e=xs, name="cross_out")

        last = layer == depth - 1
        out = _moe(xp, xs, layer, norm_ffn[layer][None], w_group, b_group, w_router, b_router,
                   w_exp_gate, w_exp_up, w_exp_down, norm_final[None], last)
        if last:
            xp, xs = out
        else:
            xc = out

    n_h = d_attn // DIFF_V_DIM
    n_ch = d_cross // CROSS_HEAD_DIM
    unpack = lambda hs, b: (jnp.stack([h[..., :SSM_STATE] for h in hs]), jnp.stack([h[..., SSM_STATE:] for h in hs]))
    s5re_p, s5im_p = unpack(s5_p, bp)
    s5re_s, s5im_s = unpack(s5_s, bs)
    return (xp.reshape(bp, seq, d), xs.reshape(bs, 1, d),
            jnp.stack(conv_p), jnp.stack(conv_s),
            s5re_p, s5im_p, s5re_s, s5im_s,
            jnp.stack(k_p).reshape(-1, bp, seq, n_h, 2, DIFF_HEAD_DIM),
            jnp.stack(v_p).reshape(-1, bp, seq, n_h, DIFF_V_DIM),
            jnp.stack(k_s).reshape(-1, bs, 1, n_h, 2, DIFF_HEAD_DIM),
            jnp.stack(v_s).reshape(-1, bs, 1, n_h, DIFF_V_DIM),
            jnp.stack(memk_p).reshape(depth, bp, n_mem, n_ch, CROSS_HEAD_DIM),
            jnp.stack(memv_p).reshape(depth, bp, n_mem, n_ch, CROSS_HEAD_DIM))
```

```python
import functools
import math

import jax
import jax.numpy as jnp
from jax import lax
from jax.experimental import pallas as pl
from jax.experimental.pallas import tpu as pltpu

F32 = jnp.float32
BF16 = jnp.bfloat16
I32 = jnp.int32
EPS = 1e-6

LANES = 128
SUBLANES = 8
VMEM_LIMIT_BYTES = 56 * 1024 * 1024

SSM_GROUP = 16
SSM_STATE = 64
S5_CHUNK = 8
DIFF_HEAD_DIM = 64
DIFF_V_DIM = 128
CROSS_HEAD_DIM = 128
N_GROUPS = 4
EXPERTS_PER_GROUP = 8
N_EXPERTS = N_GROUPS * EXPERTS_PER_GROUP
MOE_ROWS = 512
MOE_TOK_TILE = 128
PAGES_PER_STEP = 8


def _params(semantics):
    return pltpu.CompilerParams(dimension_semantics=semantics, vmem_limit_bytes=VMEM_LIMIT_BYTES)


def _rms(v, g):
    return v * lax.rsqrt(jnp.mean(v * v, axis=-1, keepdims=True) + EPS) * g


def _nt_dot(a, b):
    return lax.dot_general(a, b, (((1,), (1,)), ((), ())), preferred_element_type=F32)


def _mm_body(*refs, n_x, prologue, epilogue, n_out, direct):
    x_refs = refs[:n_x]
    w_refs = refs[n_x:2 * n_x]
    pos = 2 * n_x
    g_ref = e_ref = None
    if prologue == "norm":
        g_ref = refs[pos]
        pos += 1
    if epilogue is not None:
        e_ref = refs[pos]
        pos += 1
    o_refs = refs[pos:pos + n_out]
    xb_refs = refs[pos + n_out:]

    if not direct:
        @pl.when(pl.program_id(1) == 0)
        def _():
            for x_ref, xb_ref in zip(x_refs, xb_refs):
                v = x_ref[...].astype(F32)
                if prologue == "norm":
                    v = _rms(v, g_ref[...])
                elif prologue == "gelu":
                    v = jax.nn.gelu(v)
                xb_ref[...] = v.astype(BF16)

    acc = None
    for k in range(n_x):
        lhs = x_refs[k][...] if direct else xb_refs[k][...]
        d = jnp.dot(lhs, w_refs[k][...].astype(BF16), preferred_element_type=F32)
        acc = d if acc is None else acc + d
    if epilogue == "res":
        acc = acc + e_ref[...]
    elif epilogue == "glu":
        s = jax.nn.gelu(e_ref[...])
        acc = s * jax.nn.sigmoid(acc)
    for o_ref in o_refs:
        o_ref[...] = acc.astype(o_ref.dtype)


def _mm(xs, ws, *, m, n, layer=0, w_rows=None, col_off=0, x_off=0, prologue=None, g=None,
        epilogue=None, e=None, e_off=0, out_dtypes=(F32,), name="mm"):
    n_x = len(xs)
    w_rows = w_rows or (0,) * n_x
    tm = min(m, 1024)
    tn = min(n, 512 if m > 128 else 1024)
    assert m % tm == 0 and n % tn == 0 and col_off % tn == 0 and x_off % tm == 0 and e_off % tm == 0
    xo, eo, co = x_off // tm, e_off // tm, col_off // tn
    direct = prologue is None and all(x.dtype == BF16 for x in xs)

    in_specs, args = [], []
    for x in xs:
        in_specs.append(pl.BlockSpec((tm, x.shape[1]), lambda i, j: (i + xo, 0)))
        args.append(x)
    for x, w, rb in zip(xs, ws, w_rows):
        in_specs.append(pl.BlockSpec((None, x.shape[1], tn), lambda i, j, rb=rb: (layer, rb, j + co)))
        args.append(w)
    if prologue == "norm":
        in_specs.append(pl.BlockSpec((1, g.shape[1]), lambda i, j: (0, 0)))
        args.append(g)
    if epilogue is not None:
        in_specs.append(pl.BlockSpec((tm, tn), lambda i, j: (i + eo, j)))
        args.append(e)
    out_specs = [pl.BlockSpec((tm, tn), lambda i, j: (i, j)) for _ in out_dtypes]
    out_shape = [jax.ShapeDtypeStruct((m, n), dt) for dt in out_dtypes]
    scratch = [] if direct else [pltpu.VMEM((tm, x.shape[1]), BF16) for x in xs]
    outs = pl.pallas_call(
        functools.partial(_mm_body, n_x=n_x, prologue=prologue, epilogue=epilogue,
                          n_out=len(out_dtypes), direct=direct),
        grid=(m // tm, n // tn),
        in_specs=in_specs, out_specs=out_specs, out_shape=out_shape,
        scratch_shapes=scratch,
        compiler_params=_params(("parallel", "arbitrary")),
        name=name,
    )(*args)
    return outs[0] if len(out_dtypes) == 1 else outs


def _rec_in_body(x_ref, g_ref, wv_ref, wc_ref, wb_ref, wu_ref, cw_ref, *rest, sample, tiles_per_seq):
    if sample:
        b0_ref, b1_ref, yc_ref, u_ref, uc_ref, xb_ref = rest
    else:
        yc_ref, u_ref, tail_ref, xb_ref, carry_ref = rest
    i = pl.program_id(0)
    j = pl.program_id(1)

    @pl.when(j == 0)
    def _():
        xb_ref[...] = _rms(x_ref[...], g_ref[...]).astype(BF16)

    xb = xb_ref[...]

    def proj(w_ref):
        return jnp.dot(xb, w_ref[...].astype(BF16), preferred_element_type=F32)

    v = proj(wv_ref)
    c = proj(wc_ref)
    b = proj(wb_ref)
    u_ref[...] = proj(wu_ref)
    uc = c * v
    cw = cw_ref[...]
    if sample:
        y = cw[0:1] * b0_ref[...] + cw[1:2] * b1_ref[...] + cw[2:3] * uc
        uc_ref[...] = uc
    else:
        tm = uc.shape[0]

        @pl.when(i % tiles_per_seq == 0)
        def _():
            carry_ref[j, 0:2, :] = jnp.zeros((2, uc.shape[1]), F32)

        prev = carry_ref[j, 0:2, :]
        row = lax.broadcasted_iota(I32, uc.shape, 0)
        s1 = jnp.where(row == 0, prev[1:2], pltpu.roll(uc, 1, 0))
        s2 = jnp.where(row == 0, prev[0:1], jnp.where(row == 1, prev[1:2], pltpu.roll(uc, 2, 0)))
        y = cw[0:1] * s2 + cw[1:2] * s1 + cw[2:3] * uc
        tail = uc[tm - 2:tm, :]
        tail_ref[...] = tail
        carry_ref[j, 0:2, :] = tail
    yc_ref[...] = (b * y).astype(yc_ref.dtype)


def _rec_in(x, x_off, m, seq, g, w_in, layer, conv_w, buf=None):
    d = x.shape[1]
    c = conv_w.shape[1]
    sample = buf is not None
    tm = min(m, 1024)
    tn = 256
    nj = c // tn
    xo = x_off // tm
    assert m % tm == 0 and x_off % tm == 0 and (sample or seq % tm == 0)
    in_specs = [pl.BlockSpec((tm, d), lambda i, j: (i + xo, 0)),
                pl.BlockSpec((1, d), lambda i, j: (0, 0))]
    for k in range(4):
        in_specs.append(pl.BlockSpec((None, d, tn), lambda i, j, k=k: (layer, 0, k * nj + j)))
    in_specs.append(pl.BlockSpec((3, tn), lambda i, j: (0, j)))
    args = [x, g, w_in, w_in, w_in, w_in, conv_w]
    blk = pl.BlockSpec((tm, tn), lambda i, j: (i, j))
    out_specs = [blk, blk]
    out_shape = [jax.ShapeDtypeStruct((m, c), BF16), jax.ShapeDtypeStruct((m, c), F32)]
    scratch = [pltpu.VMEM((tm, d), BF16)]
    if sample:
        in_specs += [blk, blk]
        args += [buf[:, 0], buf[:, 1]]
        out_specs.append(blk)
        out_shape.append(jax.ShapeDtypeStruct((m, c), F32))
    else:
        out_specs.append(pl.BlockSpec((None, 2, tn), lambda i, j: (i, 0, j)))
        out_shape.append(jax.ShapeDtypeStruct((m // tm, 2, c), F32))
        scratch.append(pltpu.VMEM((nj, SUBLANES, tn), F32))
    return pl.pallas_call(
        functools.partial(_rec_in_body, sample=sample, tiles_per_seq=max(seq // tm, 1)),
        grid=(m // tm, nj),
        in_specs=in_specs, out_specs=out_specs, out_shape=out_shape,
        scratch_shapes=scratch,
        compiler_params=_params(("arbitrary", "arbitrary")),
        name="rec_in_sample" if sample else "rec_in_prompt",
    )(*args)


def _s5_block_body(sg_ref, ag_ref, eg_ref, p_ref, q_ref, s_ref, a_ref, e_ref):
    gs = p_ref.shape[0]
    a_acc = None
    for a in range(gs):
        p, q = p_ref[a], q_ref[a]
        rows = jnp.dot(p, ag_ref[a].astype(BF16), preferred_element_type=F32).astype(BF16)
        part = jnp.dot(rows, q, preferred_element_type=F32)
        a_acc = part if a_acc is None else a_acc + part
        s_ref[:, a * LANES:(a + 1) * LANES] = jnp.dot(
            p, sg_ref[a].astype(BF16), preferred_element_type=F32).astype(BF16)
        e_ref[a * LANES:(a + 1) * LANES, :] = jnp.dot(
            eg_ref[a].astype(BF16), q, preferred_element_type=F32).astype(BF16)
    a_ref[...] = a_acc.astype(BF16)


def _s5_block_tables(s_g, a_g, e_g, chunk, gs):
    n_g = s_g.shape[0]
    n_sg = n_g // gs
    wd = chunk * gs * SSM_GROUP
    r = lax.broadcasted_iota(I32, (gs, wd, LANES), 1)
    q = lax.broadcasted_iota(I32, (gs, wd, LANES), 2)
    a = lax.broadcasted_iota(I32, (gs, wd, LANES), 0)
    place = ((r // SSM_GROUP) % gs == a) & (r // (gs * SSM_GROUP) == q // SSM_GROUP) & (r % SSM_GROUP == q % SSM_GROUP)
    p = place.astype(BF16)
    grp = pl.BlockSpec((gs, LANES, LANES), lambda s: (s, 0, 0))
    out = pl.BlockSpec((None, wd, wd), lambda s: (s, 0, 0))
    shape = jax.ShapeDtypeStruct((n_sg, wd, wd), BF16)
    return pl.pallas_call(
        _s5_block_body,
        grid=(n_sg,),
        in_specs=[grp, grp, grp,
                  pl.BlockSpec((gs, wd, LANES), lambda s: (0, 0, 0)),
                  pl.BlockSpec((gs, LANES, wd), lambda s: (0, 0, 0))],
        out_specs=[out, out, out],
        out_shape=[shape, shape, shape],
        compiler_params=_params(("parallel",)),
        name="s5_tables",
    )(s_g, a_g, e_g, p, jnp.transpose(p, (0, 2, 1)))


def _s5_tables(lam_re, lam_im, log_dt, b_re, b_im, c_re, c_im, d_skip, chunk, gs):
    n_g, n_s = lam_re.shape
    dt = jnp.exp(log_dt.astype(F32))[:, None]
    ar, ai = lam_re.astype(F32) * dt, lam_im.astype(F32) * dt
    k = jnp.arange(chunk + 1, dtype=F32)[:, None, None]
    mag = jnp.exp(k * ar)
    pw_re, pw_im = mag * jnp.cos(k * ai), mag * jnp.sin(k * ai)
    xr, xi = pw_re[1] - 1.0, pw_im[1]
    den = lam_re * lam_re + lam_im * lam_im
    qr, qi = (xr * lam_re + xi * lam_im) / den, (xi * lam_re - xr * lam_im) / den
    bb_re = qr[..., None] * b_re - qi[..., None] * b_im
    bb_im = qr[..., None] * b_im + qi[..., None] * b_re
    cp_re = c_re[None] * pw_re[:, :, None, :] - c_im[None] * pw_im[:, :, None, :]
    cp_im = c_re[None] * pw_im[:, :, None, :] + c_im[None] * pw_re[:, :, None, :]
    kern = jnp.sum(cp_re[:chunk, :, :, :, None] * bb_re[None, :, None, :, :]
                   - cp_im[:chunk, :, :, :, None] * bb_im[None, :, None, :, :], axis=3)
    pos = jnp.arange(chunk)
    lag = pos[None, :] - pos[:, None]
    toe = jnp.where((lag >= 0)[:, :, None, None, None], kern[jnp.clip(lag, 0)], 0.0)
    a_t = jnp.transpose(toe, (2, 0, 4, 1, 3))
    e_t = jnp.concatenate([jnp.transpose(cp_re[1:], (1, 3, 0, 2)),
                           -jnp.transpose(cp_im[1:], (1, 3, 0, 2))], axis=1)
    rev_re, rev_im = pw_re[chunk - 1::-1][:chunk], pw_im[chunk - 1::-1][:chunk]
    pb_re = rev_re[..., None] * bb_re[None] - rev_im[..., None] * bb_im[None]
    pb_im = rev_re[..., None] * bb_im[None] + rev_im[..., None] * bb_re[None]
    s_t = jnp.concatenate([jnp.transpose(pb_re, (1, 0, 3, 2)),
                           jnp.transpose(pb_im, (1, 0, 3, 2))], axis=-1)
    lam_a = jnp.concatenate([pw_re[chunk], pw_re[chunk]], axis=-1)
    lam_b = jnp.concatenate([-pw_im[chunk], pw_im[chunk]], axis=-1)

    d_row = d_skip.astype(F32).reshape(1, -1)
    if chunk * SSM_GROUP == LANES:
        s_bd, a_bd, e_bd = _s5_block_tables(s_t.reshape(n_g, LANES, 2 * n_s), a_t.reshape(n_g, LANES, LANES),
                                            e_t.reshape(n_g, 2 * n_s, LANES), chunk, gs)
        return s_bd, a_bd, e_bd, lam_a, lam_b, d_row

    n_sg = n_g // gs
    s_t, a_t, e_t = lax.optimization_barrier((s_t, a_t, e_t))
    eye = jnp.eye(gs, dtype=F32)
    wd = chunk * gs * SSM_GROUP
    sd = gs * 2 * n_s
    s5 = jnp.transpose(s_t.reshape(n_sg, gs, chunk, SSM_GROUP, 2 * n_s), (0, 2, 1, 3, 4))
    s_bd = s5[:, :, :, :, None, :] * eye[None, None, :, None, :, None]
    a6 = jnp.transpose(a_t.reshape(n_sg, gs, chunk, SSM_GROUP, chunk, SSM_GROUP), (0, 2, 1, 3, 4, 5))
    a_bd = a6[:, :, :, :, :, None, :] * eye[None, None, :, None, None, :, None]
    e5 = e_t.reshape(n_sg, gs, 2 * n_s, chunk, SSM_GROUP)
    e_bd = e5[:, :, :, :, None, :] * eye[None, :, None, None, :, None]
    return (s_bd.reshape(n_sg, wd, sd).astype(BF16), a_bd.reshape(n_sg, wd, wd).astype(BF16),
            e_bd.reshape(n_sg, sd, wd).astype(BF16), lam_a, lam_b, d_skip.astype(F32).reshape(1, -1))


def _s5_local_body(u_ref, st_ref, o_ref):
    o_ref[...] = jnp.dot(u_ref[...].astype(BF16), st_ref[...], preferred_element_type=F32)


def _s5_scan_body(s_ref, h0_ref, la_ref, lb_ref, hc_ref, hl_ref, *, n_chunks):
    la = la_ref[...]
    lb = lb_ref[...]

    def step(c, h):
        hc_ref[c] = h
        return la * h + lb * pltpu.roll(h, SSM_STATE, 1) + s_ref[c]

    hl_ref[...] = lax.fori_loop(0, n_chunks, step, h0_ref[...])


def _s5_out_body(u_ref, at_ref, hc_ref, et_ref, d_ref, y_ref):
    u = u_ref[...]
    y = jnp.dot(u.astype(BF16), at_ref[...], preferred_element_type=F32)
    y = y + jnp.dot(hc_ref[...].astype(BF16), et_ref[...], preferred_element_type=F32)
    y_ref[...] = y + u * d_ref[...]


def _s5_prompt_body(u_ref, s_ref, a_ref, e_ref, la_ref, lb_ref, d_ref, y_ref, hl_ref,
                    ucat_ref, st_ref, sw_ref, hc_ref, *, chunk, n_c):
    gs = hl_ref.shape[0]
    for p in range(chunk):
        ucat_ref[:, p * LANES:(p + 1) * LANES] = u_ref[pl.ds(p, n_c, stride=chunk), :].astype(BF16)
    uc = ucat_ref[...]
    s_loc = jnp.dot(uc, s_ref[...], preferred_element_type=F32)
    for g in range(gs):
        blk = s_loc[:, g * LANES:(g + 1) * LANES]
        st_ref[pl.ds(g, n_c, stride=gs), :] = blk
        sw_ref[pl.ds(g, n_c, stride=gs), :] = pltpu.roll(blk, SSM_STATE, 1)
    la = la_ref[...]
    lb = lb_ref[...]

    def step(c, carry):
        h, hs = carry
        off = pl.multiple_of(c * gs, gs)
        s = st_ref[pl.ds(off, gs), :]
        ss = sw_ref[pl.ds(off, gs), :]
        st_ref[pl.ds(off, gs), :] = h
        return la * h + lb * hs + s, la * hs - lb * h + ss

    zero = jnp.zeros((gs, LANES), F32)
    hl_ref[...] = lax.fori_loop(0, n_c, step, (zero, zero), unroll=8)[0]
    for g in range(gs):
        hc_ref[:, g * LANES:(g + 1) * LANES] = st_ref[pl.ds(g, n_c, stride=gs), :].astype(BF16)
    y = jnp.dot(uc, a_ref[...], preferred_element_type=F32)
    y = y + jnp.dot(hc_ref[...], e_ref[...], preferred_element_type=F32)
    d = d_ref[...]
    for p in range(chunk):
        rows = pl.ds(p, n_c, stride=chunk)
        y_ref[rows, :] = y[:, p * LANES:(p + 1) * LANES] + u_ref[rows, :] * d


def _s5_prompt(u, bsz, seq, tables, chunk):
    s_bd, a_bd, e_bd, lam_a, lam_b, d = tables
    n_g = u.shape[1] // SSM_GROUP
    gs = LANES // SSM_GROUP
    n_blk = n_g // gs
    n_c = seq // chunk
    wd = chunk * LANES
    sd = gs * 2 * SSM_STATE
    tok = pl.BlockSpec((seq, LANES), lambda k, b: (b, k))
    wspec = lambda r, c: pl.BlockSpec((None, r, c), lambda k, b: (k, 0, 0))
    lam_spec = pl.BlockSpec((gs, 2 * SSM_STATE), lambda k, b: (k, 0))
    return pl.pallas_call(
        functools.partial(_s5_prompt_body, chunk=chunk, n_c=n_c),
        grid=(n_blk, bsz),
        in_specs=[tok, wspec(wd, sd), wspec(wd, wd), wspec(sd, wd), lam_spec, lam_spec,
                  pl.BlockSpec((1, LANES), lambda k, b: (0, k))],
        out_specs=[tok, pl.BlockSpec((None, gs, 2 * SSM_STATE), lambda k, b: (b, k, 0))],
        out_shape=[jax.ShapeDtypeStruct(u.shape, F32),
                   jax.ShapeDtypeStruct((bsz, n_g, 2 * SSM_STATE), F32)],
        scratch_shapes=[pltpu.VMEM((n_c, wd), BF16),
                        pltpu.VMEM((n_c * gs, 2 * SSM_STATE), F32),
                        pltpu.VMEM((n_c * gs, 2 * SSM_STATE), F32),
                        pltpu.VMEM((n_c, sd), BF16)],
        compiler_params=_params(("parallel", "parallel")),
        name="s5_prompt",
    )(u, s_bd, a_bd, e_bd, lam_a, lam_b, d)


def _s5_step(u, tables, h0, gs):
    s_t, a_t, e_t, lam_a, lam_b, d_t = tables
    bsz, seq, chunk = u.shape[0], 1, 1
    n_g = u.shape[1] // SSM_GROUP
    n_sg = n_g // gs
    n_c = 1
    rows = bsz
    w = gs * SSM_GROUP
    sw = gs * 2 * SSM_STATE
    up = u

    s_loc = pl.pallas_call(
        _s5_local_body,
        grid=(n_sg,),
        in_specs=[pl.BlockSpec((rows, w), lambda s: (0, s)),
                  pl.BlockSpec((None, w, sw), lambda s: (s, 0, 0))],
        out_specs=pl.BlockSpec((rows, sw), lambda s: (0, s)),
        out_shape=jax.ShapeDtypeStruct((rows, n_g * 2 * SSM_STATE), F32),
        compiler_params=_params(("parallel",)),
        name="s5_local",
    )(up, s_t)

    r2 = bsz * n_g
    rb = min(r2, 1024 if n_c == 1 else 64)
    lane = 2 * SSM_STATE
    la = jnp.broadcast_to(lam_a[None], (bsz, n_g, lane)).reshape(r2, lane)
    lb = jnp.broadcast_to(lam_b[None], (bsz, n_g, lane)).reshape(r2, lane)
    hc, h_last = pl.pallas_call(
        functools.partial(_s5_scan_body, n_chunks=n_c),
        grid=(r2 // rb,),
        in_specs=[pl.BlockSpec((n_c, rb, lane), lambda r: (0, r, 0)),
                  pl.BlockSpec((rb, lane), lambda r: (r, 0)),
                  pl.BlockSpec((rb, lane), lambda r: (r, 0)),
                  pl.BlockSpec((rb, lane), lambda r: (r, 0))],
        out_specs=[pl.BlockSpec((n_c, rb, lane), lambda r: (0, r, 0)),
                   pl.BlockSpec((rb, lane), lambda r: (r, 0))],
        out_shape=[jax.ShapeDtypeStruct((n_c, r2, lane), F32),
                   jax.ShapeDtypeStruct((r2, lane), F32)],
        compiler_params=_params(("parallel",)),
        name="s5_scan",
    )(s_loc.reshape(n_c, r2, lane), h0.reshape(r2, lane), la, lb)

    yp = pl.pallas_call(
        _s5_out_body,
        grid=(n_sg,),
        in_specs=[pl.BlockSpec((rows, w), lambda s: (0, s)),
                  pl.BlockSpec((None, w, w), lambda s: (s, 0, 0)),
                  pl.BlockSpec((rows, sw), lambda s: (0, s)),
                  pl.BlockSpec((None, sw, w), lambda s: (s, 0, 0)),
                  pl.BlockSpec((1, w), lambda s: (0, s))],
        out_specs=pl.BlockSpec((rows, w), lambda s: (0, s)),
        out_shape=jax.ShapeDtypeStruct((rows, n_g * chunk * SSM_GROUP), F32),
        compiler_params=_params(("parallel",)),
        name="s5_out",
    )(up, a_t, hc.reshape(rows, n_g * lane), e_t, d_t)
    return yp, h_last.reshape(bsz, n_g, lane)


def _cross_prompt_body(q_ref, k_ref, v_ref, o_ref, *, n_heads):
    scale = CROSS_HEAD_DIM ** -0.5
    for h in range(n_heads):
        sl = slice(h * CROSS_HEAD_DIM, (h + 1) * CROSS_HEAD_DIM)
        s = _nt_dot(q_ref[:, sl], k_ref[:, sl].astype(BF16)) * scale
        e = jnp.exp(s - jnp.max(s, axis=-1, keepdims=True))
        l = jnp.sum(e, axis=-1, keepdims=True)
        o = jnp.dot(e.astype(BF16), v_ref[:, sl].astype(BF16), preferred_element_type=F32)
        o_ref[:, sl] = (o / l).astype(o_ref.dtype)


def _cross_prompt(q, mem_k, mem_v, bsz, seq):
    dc = q.shape[1]
    n_mem = mem_k.shape[0] // bsz
    tq = min(seq, 1024)
    nq = seq // tq
    return pl.pallas_call(
        functools.partial(_cross_prompt_body, n_heads=dc // CROSS_HEAD_DIM),
        grid=(bsz, nq),
        in_specs=[pl.BlockSpec((tq, dc), lambda b, i: (b * nq + i, 0)),
                  pl.BlockSpec((n_mem, dc), lambda b, i: (b, 0)),
                  pl.BlockSpec((n_mem, dc), lambda b, i: (b, 0))],
        out_specs=pl.BlockSpec((tq, dc), lambda b, i: (b * nq + i, 0)),
        out_shape=jax.ShapeDtypeStruct((bsz * seq, dc), BF16),
        compiler_params=_params(("parallel", "parallel")),
        name="cross_prompt",
    )(q, mem_k, mem_v)


def _cross_sample_body(q_ref, k_ref, v_ref, ones_ref, o_ref):
    bs, rows, dh = k_ref.shape
    n_heads = q_ref.shape[1]
    scale = CROSS_HEAD_DIM ** -0.5
    q = q_ref[...]
    row_head = lax.broadcasted_iota(I32, (bs, rows, dh), 1) % n_heads
    qt = jnp.broadcast_to(q[:, 0:1, :], (bs, rows, dh))
    for h in range(1, n_heads):
        qt = jnp.where(row_head == h, q[:, h:h + 1, :], qt)
    kq = (k_ref[...] * qt).reshape(bs * rows, dh).astype(BF16)
    s = jnp.dot(kq, ones_ref[...], preferred_element_type=F32) * scale
    s = s.reshape(bs, rows // SUBLANES, SUBLANES, dh)

    def over_memory(x, reduce, combine):
        r = reduce(x, axis=1)
        shift = n_heads
        while shift < SUBLANES:
            r = combine(r, pltpu.roll(r, shift, 1))
            shift *= 2
        return r

    m = over_memory(s, jnp.max, jnp.maximum)
    e = jnp.exp(s - m[:, None])
    p = e / over_memory(e, jnp.sum, jnp.add)[:, None]
    o_ref[...] = over_memory(p * v_ref[...].reshape(s.shape), jnp.sum, jnp.add)


def _cross_sample(q, cache_k, cache_v, layer):
    bsz, dc = q.shape
    n_heads = dc // CROSS_HEAD_DIM
    rows = cache_k.shape[2]
    bs = SUBLANES
    kv_spec = pl.BlockSpec((None, bs, rows, CROSS_HEAD_DIM), lambda i: (layer, i, 0, 0))
    out = pl.pallas_call(
        _cross_sample_body,
        grid=(bsz // bs,),
        in_specs=[pl.BlockSpec((bs, n_heads, CROSS_HEAD_DIM), lambda i: (i, 0, 0)), kv_spec, kv_spec,
                  pl.BlockSpec((CROSS_HEAD_DIM, LANES), lambda i: (0, 0))],
        out_specs=pl.BlockSpec((bs, SUBLANES, CROSS_HEAD_DIM), lambda i: (i, 0, 0)),
        out_shape=jax.ShapeDtypeStruct((bsz, SUBLANES, CROSS_HEAD_DIM), F32),
        compiler_params=_params(("parallel",)),
        name="cross_sample",
    )(q.reshape(bsz, n_heads, CROSS_HEAD_DIM), cache_k, cache_v, jnp.ones((CROSS_HEAD_DIM, LANES), BF16))
    return out[:, :n_heads].reshape(bsz, dc).astype(BF16)


def _softmax_step(s, v, m, l, acc):
    m_new = jnp.maximum(m, jnp.max(s, axis=-1, keepdims=True))
    corr = jnp.exp(m - m_new)
    p = jnp.exp(s - m_new)
    l = l * corr + jnp.sum(p, axis=-1, keepdims=True)
    acc = acc * corr + jnp.dot(p.astype(BF16), v, preferred_element_type=F32)
    return m_new, l, acc


def _diff_prompt_body(lam_ref, q_ref, k_ref, v_ref, g_ref, o_ref, *, tq, lambda_init):
    qi = pl.program_id(2)
    q = q_ref[...] * DIFF_HEAD_DIM ** -0.5
    lane = lax.broadcasted_iota(I32, q.shape, 1)
    zero = jnp.zeros_like(q)
    q1 = jnp.where(lane < DIFF_HEAD_DIM, q, zero)
    q2 = jnp.where(lane >= DIFF_HEAD_DIM, q, zero)

    def chunk(kc, carry, masked):
        off = pl.multiple_of(kc * tq, tq)
        k = k_ref[pl.ds(off, tq), :]
        v = v_ref[pl.ds(off, tq), :]
        s1 = _nt_dot(q1, k)
        s2 = _nt_dot(q2, k)
        if masked:
            keep = (lax.broadcasted_iota(I32, s1.shape, 1) <= lax.broadcasted_iota(I32, s1.shape, 0))
            s1 = jnp.where(keep, s1, -jnp.inf)
            s2 = jnp.where(keep, s2, -jnp.inf)
        m1, l1, a1, m2, l2, a2 = carry
        m1, l1, a1 = _softmax_step(s1, v, m1, l1, a1)
        m2, l2, a2 = _softmax_step(s2, v, m2, l2, a2)
        return m1, l1, a1, m2, l2, a2

    col = lambda val: jnp.full((tq, 1), val, F32)
    acc0 = jnp.zeros((tq, DIFF_V_DIM), F32)
    carry = (col(-jnp.inf), col(0.0), acc0, col(-jnp.inf), col(0.0), acc0)
    carry = lax.fori_loop(0, qi, lambda kc, cr: chunk(kc, cr, False), carry)
    m1, l1, a1, m2, l2, a2 = chunk(qi, carry, True)
    w = a1 / l1 - lam_ref[0] * (a2 / l2)
    o_ref[...] = (_rms(w, g_ref[...]) * (1.0 - lambda_init)).astype(o_ref.dtype)


def _diff_prompt(q, k, v, lam, subln, bsz, seq, lambda_init):
    n_heads = q.shape[1] // DIFF_V_DIM
    tq = min(seq, 512)
    nq = seq // tq
    kv_spec = pl.BlockSpec((seq, DIFF_V_DIM), lambda b, h, i: (b, h))
    q_spec = pl.BlockSpec((tq, DIFF_V_DIM), lambda b, h, i: (b * nq + i, h))
    return pl.pallas_call(
        functools.partial(_diff_prompt_body, tq=tq, lambda_init=lambda_init),
        grid=(bsz, n_heads, nq),
        in_specs=[pl.BlockSpec(memory_space=pltpu.SMEM), q_spec, kv_spec, kv_spec,
                  pl.BlockSpec((1, DIFF_V_DIM), lambda b, h, i: (0, 0))],
        out_specs=q_spec,
        out_shape=jax.ShapeDtypeStruct(q.shape, BF16),
        compiler_params=_params(("parallel", "parallel", "arbitrary")),
        name="diff_prompt",
    )(lam, q, k, v, subln)


def _diff_sample_body(pt_ref, lam_ref, q_ref, kn_ref, vn_ref, g_ref, x_ref, *rest, n_heads, lambda_init):
    k_refs = rest[:PAGES_PER_STEP]
    v_refs = rest[PAGES_PER_STEP:2 * PAGES_PER_STEP]
    o_ref, qb_ref, m_ref, l_ref, acc_ref = rest[2 * PAGES_PER_STEP:]
    step = pl.program_id(1)
    n_rows = 2 * n_heads
    width = n_heads * DIFF_V_DIM
    row = lax.broadcasted_iota(I32, (n_rows, width), 0)
    col = lax.broadcasted_iota(I32, (n_rows, width), 1)
    own = (col // DIFF_V_DIM == row % n_heads) & ((col // DIFF_HEAD_DIM) % 2 == row // n_heads)

    @pl.when(step == 0)
    def _():
        qs = q_ref[...] * DIFF_HEAD_DIM ** -0.5
        qb_ref[...] = jnp.where(own, jnp.broadcast_to(qs, (n_rows, width)), 0.0).astype(BF16)
        m_ref[...] = jnp.full(m_ref.shape, -jnp.inf, F32)
        l_ref[...] = jnp.zeros(l_ref.shape, F32)
        acc_ref[...] = jnp.zeros(acc_ref.shape, F32)

    qb = qb_ref[...]
    m, l, acc = m_ref[...], l_ref[...], acc_ref[...]
    pe_shape = (n_rows, x_ref.shape[1])
    mine = (lax.broadcasted_iota(I32, pe_shape, 1) % n_heads
            == lax.broadcasted_iota(I32, pe_shape, 0) % n_heads)
    for k_ref, v_ref in zip(k_refs, v_refs):
        s = jnp.dot(qb, k_ref[...].astype(BF16), preferred_element_type=F32)
        m_new = jnp.maximum(m, jnp.max(s, axis=-1, keepdims=True))
        corr = jnp.exp(m - m_new)
        p = jnp.exp(s - m_new)
        l = l * corr + jnp.sum(p, axis=-1, keepdims=True)
        pe = jnp.dot(p.astype(BF16), x_ref[...], preferred_element_type=F32)
        pe = jnp.where(mine, pe, 0.0).astype(BF16)
        acc = acc * corr + jnp.dot(pe, v_ref[...].astype(BF16), preferred_element_type=F32)
        m = m_new
    m_ref[...], l_ref[...], acc_ref[...] = m, l, acc

    @pl.when(step == pl.num_programs(1) - 1)
    def _():
        s_new = jnp.sum(qb.astype(F32) * kn_ref[...], axis=-1, keepdims=True)
        m_new = jnp.maximum(m, s_new)
        corr = jnp.exp(m - m_new)
        p = jnp.exp(s_new - m_new)
        l_fin = l * corr + p
        vn = vn_ref[...]
        o = (acc * corr + p * jnp.concatenate([vn, vn], axis=0)) / l_fin
        w = o[:n_heads] - lam_ref[0] * o[n_heads:]
        o_ref[...] = (_rms(w, g_ref[...]) * (1.0 - lambda_init)).astype(o_ref.dtype)


def _diff_sample(q, k_new, v_new, cache_kt, cache_v, layer, page_table, lam, subln, lambda_init):
    bsz, width = q.shape
    n_heads = width // DIFF_V_DIM
    page = cache_kt.shape[3]
    n_pages = page_table.shape[1]
    n_steps = n_pages // PAGES_PER_STEP
    assert n_pages % PAGES_PER_STEP == 0
    row_spec = pl.BlockSpec((None, 1, width), lambda b, s, pt: (b, 0, 0))
    head_spec = pl.BlockSpec((None, n_heads, DIFF_V_DIM), lambda b, s, pt: (b, 0, 0))

    def page_spec(k, rows, cols):
        return pl.BlockSpec((None, None, rows, cols),
                            lambda b, s, pt, k=k: (layer, pt[b * n_pages + s * PAGES_PER_STEP + k], 0, 0))

    expand = (lax.broadcasted_iota(I32, (page, page * n_heads), 1) // n_heads
              == lax.broadcasted_iota(I32, (page, page * n_heads), 0)).astype(BF16)
    in_specs = [pl.BlockSpec(memory_space=pltpu.SMEM), row_spec, row_spec, head_spec,
                pl.BlockSpec((1, DIFF_V_DIM), lambda b, s, pt: (0, 0)),
                pl.BlockSpec((page, page * n_heads), lambda b, s, pt: (0, 0))]
    in_specs += [page_spec(k, width, page) for k in range(PAGES_PER_STEP)]
    in_specs += [page_spec(k, page * n_heads, DIFF_V_DIM) for k in range(PAGES_PER_STEP)]
    grid_spec = pltpu.PrefetchScalarGridSpec(
        num_scalar_prefetch=1,
        grid=(bsz, n_steps),
        in_specs=in_specs,
        out_specs=head_spec,
        scratch_shapes=[pltpu.VMEM((2 * n_heads, width), BF16),
                        pltpu.VMEM((2 * n_heads, 1), F32),
                        pltpu.VMEM((2 * n_heads, 1), F32),
                        pltpu.VMEM((2 * n_heads, DIFF_V_DIM), F32)],
    )
    out = pl.pallas_call(
        functools.partial(_diff_sample_body, n_heads=n_heads, lambda_init=lambda_init),
        grid_spec=grid_spec,
        out_shape=jax.ShapeDtypeStruct((bsz, n_heads, DIFF_V_DIM), BF16),
        compiler_params=_params(("parallel", "arbitrary")),
        name="diff_sample",
    )(page_table.reshape(-1), lam, q.reshape(bsz, 1, width), k_new.reshape(bsz, 1, width),
      v_new.reshape(bsz, n_heads, DIFF_V_DIM), subln, expand,
      *([cache_kt] * PAGES_PER_STEP), *([cache_v] * PAGES_PER_STEP))
    return out.reshape(bsz, width)


def _split_bf16(v):
    hi = v.astype(BF16)
    return hi, (v - hi.astype(F32)).astype(BF16)


def _to_slabs(slab_ref, value):
    t, width = value.shape
    n_slab = width // LANES
    for s in range(n_slab):
        slab_ref[pl.ds(s, t, stride=n_slab), :] = value[:, s * LANES:(s + 1) * LANES].astype(slab_ref.dtype)


def _slab_column(slab_ref, s, t, n_slab, first_row=0):
    return slab_ref[pl.ds(first_row + s, t, stride=n_slab), :]


def _start_slab_copies(src_ref, dst_ref, sem, index_of, n_copies, n_slab, unroll=8, priority=0):
    def body(it, carry):
        for u in range(unroll):
            r = it * unroll + u
            src = pl.multiple_of(index_of(r) * n_slab, n_slab)
            dst = pl.multiple_of(r * n_slab, n_slab)
            pltpu.make_async_copy(src_ref.at[pl.ds(src, n_slab)], dst_ref.at[pl.ds(dst, n_slab)],
                                  sem).start(priority=priority)
        return carry

    lax.fori_loop(0, n_copies // unroll, body, 0)


def _wait_slab_copies(src_ref, dst_ref, sem):
    pltpu.make_async_copy(src_ref.at[pl.ds(0, dst_ref.shape[0])], dst_ref, sem).wait()


def _router_body(xp_ref, xs_ref, g_ref, w_ref, b_ref, hf_ref, eid_ref, gate_ref, *, n_prompt_tiles):
    i = pl.program_id(0)
    x = jnp.where(i < n_prompt_tiles, xp_ref[...], xs_ref[...])
    hf = _rms(x, g_ref[...])
    _to_slabs(hf_ref, hf)
    h_hi, h_lo = _split_bf16(hf)
    w_hi, w_lo = _split_bf16(w_ref[...])
    dot = lambda a, b: jnp.dot(a, b, preferred_element_type=F32)
    logits = dot(h_hi, w_hi) + (dot(h_lo, w_hi) + dot(h_hi, w_lo)) + b_ref[...]
    lane = lax.broadcasted_iota(I32, logits.shape, 1)
    lane_f = lane.astype(F32)
    first = lambda hit: jnp.min(jnp.where(hit, lane_f, float(LANES)), axis=-1, keepdims=True).astype(I32)
    gl = jnp.where(lane < N_GROUPS, logits, -jnp.inf)
    g_max = jnp.max(gl, axis=-1, keepdims=True)
    g_idx = first(gl == g_max)
    g_w = 1.0 / jnp.sum(jnp.exp(gl - g_max), axis=-1, keepdims=True)
    lo = N_GROUPS + EXPERTS_PER_GROUP * g_idx
    in_group = (lane >= lo) & (lane < lo + EXPERTS_PER_GROUP)
    el = jnp.where(in_group, logits, -jnp.inf)
    ee = jnp.exp(el - jnp.max(el, axis=-1, keepdims=True))
    prob = jnp.where(in_group, ee / jnp.sum(ee, axis=-1, keepdims=True), -1.0)
    p1 = jnp.max(prob, axis=-1, keepdims=True)
    i1 = first(prob == p1)
    rest = jnp.where(lane == i1, -1.0, prob)
    p2 = jnp.max(rest, axis=-1, keepdims=True)
    i2 = first(rest == p2)
    denom = p1 + p2
    eid_ref[...] = jnp.where(lane == 0, i1 - N_GROUPS, jnp.where(lane == 1, i2 - N_GROUPS, 0))
    gate_ref[...] = jnp.where(lane == 0, g_w * p1 / denom, jnp.where(lane == 1, g_w * p2 / denom, 0.0))


def _router(xp, xs, g, w_cat, b_cat):
    d = xp.shape[1]
    t = MOE_TOK_TILE
    n_p, n_s = xp.shape[0] // t, xs.shape[0] // t
    n = xp.shape[0] + xs.shape[0]
    tile = lambda cols: pl.BlockSpec((t, cols), lambda i: (i, 0))
    return pl.pallas_call(
        functools.partial(_router_body, n_prompt_tiles=n_p),
        grid=(n_p + n_s,),
        in_specs=[pl.BlockSpec((t, d), lambda i: (jnp.minimum(i, n_p - 1), 0)),
                  pl.BlockSpec((t, d), lambda i: (jnp.maximum(i - n_p, 0), 0)),
                  pl.BlockSpec((1, d), lambda i: (0, 0)),
                  pl.BlockSpec((d, LANES), lambda i: (0, 0)),
                  pl.BlockSpec((1, LANES), lambda i: (0, 0))],
        out_specs=[pl.BlockSpec((t * (d // LANES), LANES), lambda i: (i, 0)), tile(LANES), tile(LANES)],
        out_shape=[jax.ShapeDtypeStruct((n * (d // LANES), LANES), F32),
                   jax.ShapeDtypeStruct((n, LANES), I32),
                   jax.ShapeDtypeStruct((n, LANES), F32)],
        compiler_params=_params(("parallel",)),
        name="moe_router",
    )(xp, xs, g, w_cat, b_cat)


EXPERT_UP_CHUNK = 256
EXPERT_DOWN_CHUNK = 1024


def _experts_body(be_ref, nblk_ref, tok_ref, src_ref, wg_ref, wu_ref, wd_ref, o_ref,
                  xin_ref, xb_ref, hb_ref, sems, *, n_up, n_down):
    blk = pl.program_id(0)
    c = pl.program_id(1)
    rows, d = xb_ref.shape
    n_slab = d // LANES
    n_used = nblk_ref[0]
    buf = blk % 2

    def fetch(b, into):
        _start_slab_copies(src_ref, xin_ref.at[into], sems.at[into],
                           lambda r: tok_ref[b * rows + r], rows, n_slab, priority=1)

    @pl.when(blk < n_used)
    def _():
        @pl.when(c == 0)
        def _():
            @pl.when(blk == 0)
            def _():
                fetch(0, 0)

            _wait_slab_copies(src_ref, xin_ref.at[buf], sems.at[buf])
            for s in range(n_slab):
                xb_ref[:, s * LANES:(s + 1) * LANES] = _slab_column(xin_ref.at[buf], s, rows, n_slab).astype(BF16)

        @pl.when((c == 1) & (blk + 1 < n_used))
        def _():
            fetch(blk + 1, 1 - buf)

        @pl.when(c < n_up)
        def _():
            xb = xb_ref[...]
            gate = jnp.dot(xb, wg_ref[...].astype(BF16), preferred_element_type=F32)
            up = jnp.dot(xb, wu_ref[...].astype(BF16), preferred_element_type=F32)
            hb_ref[c] = (jax.nn.silu(gate) * up).astype(BF16)

        ck = hb_ref.shape[2]
        per_step = wd_ref.shape[1] // LANES
        for j in range(n_down):
            @pl.when(c == n_up + j)
            def _():
                wd = wd_ref[...].astype(BF16)
                y = None
                for k in range(n_up):
                    part = jnp.dot(hb_ref[k], wd[k * ck:(k + 1) * ck, :], preferred_element_type=F32)
                    y = part if y is None else y + part
                for s in range(per_step):
                    o_ref[pl.ds(j * per_step + s, rows, stride=n_slab), :] = y[:, s * LANES:(s + 1) * LANES]

    @pl.when((blk >= n_used) & (c == 0))
    def _():
        o_ref[...] = jnp.zeros(o_ref.shape, o_ref.dtype)


def _experts(src, slot_tok, block_e, n_used, w_gate, w_up, w_down, layer):
    d, de = w_gate.shape[2], w_gate.shape[3]
    n_slab = d // LANES
    n_blocks = slot_tok.shape[0] // MOE_ROWS
    n_up = de // EXPERT_UP_CHUNK
    n_down = d // EXPERT_DOWN_CHUNK
    assert n_up >= 2

    def expert(b, be, nb):
        return be[jnp.minimum(b, nb[0] - 1)]

    up_spec = pl.BlockSpec((None, None, d, EXPERT_UP_CHUNK),
                           lambda b, c, be, nb, tok: (layer, expert(b, be, nb), 0, jnp.minimum(c, n_up - 1)))
    grid_spec = pltpu.PrefetchScalarGridSpec(
        num_scalar_prefetch=3,
        grid=(n_blocks, n_up + n_down),
        in_specs=[pl.BlockSpec(memory_space=pl.ANY), up_spec, up_spec,
                  pl.BlockSpec((None, None, de, EXPERT_DOWN_CHUNK),
                               lambda b, c, be, nb, tok: (layer, expert(b, be, nb), 0, jnp.maximum(c - n_up, 0)))],
        out_specs=pl.BlockSpec((MOE_ROWS * n_slab, LANES), lambda b, c, be, nb, tok: (b, 0)),
        scratch_shapes=[pltpu.VMEM((2, MOE_ROWS * n_slab, LANES), F32),
                        pltpu.VMEM((MOE_ROWS, d), BF16),
                        pltpu.VMEM((n_up, MOE_ROWS, EXPERT_UP_CHUNK), BF16),
                        pltpu.SemaphoreType.DMA((2,))],
    )
    return pl.pallas_call(
        functools.partial(_experts_body, n_up=n_up, n_down=n_down),
        grid_spec=grid_spec,
        out_shape=jax.ShapeDtypeStruct((n_blocks * MOE_ROWS * n_slab, LANES), F32),
        compiler_params=_params(("arbitrary", "arbitrary")),
        name="moe_experts",
    )(block_e, n_used, slot_tok, src, w_gate, w_up, w_down)


def _combine_body(dest_ref, xp_ref, xs_ref, gate_ref, y_ref, g_ref, *rest, n_prompt_tiles, final_norm):
    if final_norm:
        op_ref, os_ref, buf_ref, sem = rest
    else:
        o_ref, buf_ref, sem = rest
    i = pl.program_id(0)
    t, d = xp_ref.shape
    n_slab = d // LANES
    cur = i % 2

    def fetch(tile, into):
        _start_slab_copies(y_ref, buf_ref.at[into], sem.at[into],
                           lambda a: dest_ref[tile * 2 * t + a], 2 * t, n_slab)

    @pl.when(i == 0)
    def _():
        fetch(0, 0)

    @pl.when(i + 1 < pl.num_programs(0))
    def _():
        fetch(i + 1, 1 - cur)

    buf = buf_ref.at[cur]
    _wait_slab_copies(y_ref, buf, sem.at[cur])
    x = jnp.where(i < n_prompt_tiles, xp_ref[...], xs_ref[...])
    gates = gate_ref[...]
    g0, g1 = gates[:, 0:1], gates[:, 1:2]
    pieces = []
    for s in range(n_slab):
        y0 = _slab_column(buf, s, t, 2 * n_slab)
        y1 = _slab_column(buf, s, t, 2 * n_slab, first_row=n_slab)
        pieces.append(x[:, s * LANES:(s + 1) * LANES] + (y0 * g0 + y1 * g1))
    out = jnp.concatenate(pieces, axis=1)
    if final_norm:
        out = _rms(out, g_ref[...])

        @pl.when(i < n_prompt_tiles)
        def _():
            op_ref[...] = out

        @pl.when(i >= n_prompt_tiles)
        def _():
            os_ref[...] = out
    else:
        o_ref[...] = out


def _combine(xp, xs, gates, y_sorted, dest, g_final, final_norm):
    d = xp.shape[1]
    t = MOE_TOK_TILE
    n_p, n_s = xp.shape[0] // t, xs.shape[0] // t
    n = xp.shape[0] + xs.shape[0]
    p_map = lambda i, dst: (jnp.minimum(i, n_p - 1), 0)
    s_map = lambda i, dst: (jnp.maximum(i - n_p, 0), 0)
    if final_norm:
        out_specs = [pl.BlockSpec((t, d), p_map), pl.BlockSpec((t, d), s_map)]
        out_shape = [jax.ShapeDtypeStruct(xp.shape, F32), jax.ShapeDtypeStruct(xs.shape, F32)]
    else:
        out_specs = pl.BlockSpec((t, d), lambda i, dst: (i, 0))
        out_shape = jax.ShapeDtypeStruct((n, d), F32)
    grid_spec = pltpu.PrefetchScalarGridSpec(
        num_scalar_prefetch=1,
        grid=(n_p + n_s,),
        in_specs=[pl.BlockSpec((t, d), p_map), pl.BlockSpec((t, d), s_map),
                  pl.BlockSpec((t, LANES), lambda i, dst: (i, 0)),
                  pl.BlockSpec(memory_space=pl.ANY),
                  pl.BlockSpec((1, d), lambda i, dst: (0, 0))],
        out_specs=out_specs,
        scratch_shapes=[pltpu.VMEM((2, 2 * t * (d // LANES), LANES), F32), pltpu.SemaphoreType.DMA((2,))],
    )
    return pl.pallas_call(
        functools.partial(_combine_body, n_prompt_tiles=n_p, final_norm=final_norm),
        grid_spec=grid_spec,
        out_shape=out_shape,
        compiler_params=_params(("arbitrary",)),
        name="moe_combine",
    )(dest, xp, xs, gates, y_sorted, g_final)


def _moe(xp, xs, layer, g_ffn, w_group, b_group, w_router, b_router, w_gate, w_up, w_down, g_final, final_norm):
    d = xp.shape[1]
    n_tok = xp.shape[0] + xs.shape[0]
    pad = LANES - N_GROUPS - N_EXPERTS
    w_cat = jnp.concatenate([w_group[layer], w_router[layer], jnp.zeros((d, pad), F32)], axis=1)
    b_cat = jnp.concatenate([b_group[layer], b_router[layer], jnp.zeros((pad,), F32)])[None]
    hf, eid, gates = _router(xp, xs, g_ffn, w_cat, b_cat)

    e_flat = eid[:, :2].reshape(-1)
    n_asg = e_flat.shape[0]
    onehot = (e_flat[:, None] == jnp.arange(N_EXPERTS, dtype=I32)[None, :]).astype(I32)
    csum = jnp.cumsum(onehot, axis=0)
    rank = jnp.sum((csum - onehot) * onehot, axis=1)
    counts = csum[-1]
    padded = ((counts + MOE_ROWS - 1) // MOE_ROWS) * MOE_ROWS
    pends = jnp.cumsum(padded)
    pstarts = pends - padded
    dest = (jnp.sum(onehot * pstarts[None, :], axis=1) + rank).astype(I32)
    n_blocks = -(-n_asg // MOE_ROWS) + N_EXPERTS
    n_slots = n_blocks * MOE_ROWS
    slot_tok = jnp.zeros((n_slots,), I32).at[dest].set(jnp.arange(n_asg, dtype=I32) // 2)
    block_start = jnp.arange(n_blocks, dtype=I32) * MOE_ROWS
    block_e = jnp.minimum(jnp.sum((block_start[:, None] >= pends[None, :]).astype(I32), axis=1),
                          N_EXPERTS - 1).astype(I32)
    n_used = (pends[-1] // MOE_ROWS).astype(I32).reshape(1)

    y_sorted = _experts(hf, slot_tok, block_e, n_used, w_gate, w_up, w_down, layer)
    return _combine(xp, xs, gates, y_sorted, dest, g_final, final_norm)


def kernel(x_prompt, x_sample, state_conv, state_s5_re, state_s5_im, cache_k, cache_v, cache_mem_k, cache_mem_v, page_table, mem_prompt, norm_mix, norm_cross, norm_ffn, norm_final, w_in_rec, conv_w, ssm_lambda_re, ssm_lambda_im, ssm_log_dt, ssm_b_re, ssm_b_im, ssm_c_re, ssm_c_im, ssm_d, ssm_w_glu, w_out_rec, w_qkv, diff_lambda_q1, diff_lambda_k1, diff_lambda_q2, diff_lambda_k2, diff_subln, w_o_attn, w_cross_q, w_cross_k, w_cross_v, w_cross_o, w_group, b_group, w_router, b_router, w_exp_gate, w_exp_up, w_exp_down):
    bp, seq, d = x_prompt.shape
    bs = x_sample.shape[0]
    depth = norm_mix.shape[0]
    n_p = bp * seq
    d_conv = conv_w.shape[2]
    n_g = ssm_lambda_re.shape[1]
    n_mem = mem_prompt.shape[1]
    d_cross = w_cross_q.shape[2]
    d_attn = w_o_attn.shape[1] if w_o_attn.shape[0] else d
    lane2 = 2 * SSM_STATE

    xc = None
    xp = x_prompt.reshape(n_p, d)
    xs = x_sample.reshape(bs, d)
    mem = mem_prompt.reshape(bp * n_mem, d)

    conv_p, conv_s, s5_p, s5_s = [], [], [], []
    k_p, v_p, k_s, v_s, memk_p, memv_p = [], [], [], [], [], []

    for layer in range(depth):
        g_mix = norm_mix[layer][None]
        src_p, off_p = (xp, 0) if xc is None else (xc, 0)
        src_s, off_s = (xs, 0) if xc is None else (xc, n_p)
        if layer % 2 == 0:
            r = layer // 2
            yc_p, u_p, tails = _rec_in(src_p, off_p, n_p, seq, g_mix, w_in_rec, r, conv_w[r])
            yc_s, u_s, uc_s = _rec_in(src_s, off_s, bs, 1, g_mix, w_in_rec, r, conv_w[r], buf=state_conv[r])
            tps = tails.shape[0] // bp
            conv_p.append(tails[tps - 1::tps])
            conv_s.append(jnp.stack([state_conv[r][:, 1], uc_s], axis=1))
            ssm = (ssm_lambda_re[r], ssm_lambda_im[r], ssm_log_dt[r], ssm_b_re[r], ssm_b_im[r],
                   ssm_c_re[r], ssm_c_im[r], ssm_d[r])
            y_p, h_p = _s5_prompt(u_p, bp, seq, _s5_tables(*ssm, S5_CHUNK, LANES // SSM_GROUP), S5_CHUNK)
            h0_s = jnp.concatenate([state_s5_re[r], state_s5_im[r]], axis=-1).reshape(bs, n_g * lane2)
            y_s, h_s = _s5_step(u_s, _s5_tables(*ssm, 1, SUBLANES), h0_s, SUBLANES)
            s5_p.append(h_p)
            s5_s.append(h_s)
            new = []
            for yc, y, m, src, off in ((yc_p, y_p, n_p, src_p, off_p), (yc_s, y_s, bs, src_s, off_s)):
                y_ssm = _mm([y], [ssm_w_glu], m=m, n=d_conv, layer=r, prologue="gelu", epilogue="glu", e=y,
                            out_dtypes=(BF16,), name="s5_glu")
                new.append(_mm([yc, y_ssm], [w_out_rec, w_out_rec], m=m, n=d, layer=r, w_rows=(0, 1),
                               epilogue="res", e=src, e_off=off, name="rec_out"))
            xp, xs = new
        else:
            a = layer // 2
            lambda_init = 0.8 - 0.6 * math.exp(-0.3 * layer)
            lam = (jnp.exp(jnp.sum(diff_lambda_q1[a] * diff_lambda_k1[a]))
                   - jnp.exp(jnp.sum(diff_lambda_q2[a] * diff_lambda_k2[a])) + lambda_init).reshape(1)
            subln = diff_subln[a][None]
            qkv = []
            for m, src, off in ((n_p, src_p, off_p), (bs, src_s, off_s)):
                sample = m == bs
                proj = functools.partial(_mm, [src], [w_qkv], m=m, n=d_attn, layer=a, x_off=off,
                                         prologue="norm", g=g_mix)
                q = proj(col_off=0, out_dtypes=(F32,) if sample else (BF16,), name="attn_q")
                if sample:
                    k = proj(col_off=d_attn, name="attn_k")
                    v = proj(col_off=2 * d_attn, name="attn_v")
                    qkv.append((q, k, v))
                else:
                    k, kb = proj(col_off=d_attn, out_dtypes=(F32, BF16), name="attn_k")
                    v, vb = proj(col_off=2 * d_attn, out_dtypes=(F32, BF16), name="attn_v")
                    qkv.append((q, k, v, kb, vb))
            q, k, v, kb, vb = qkv[0]
            k_p.append(k)
            v_p.append(v)
            o_p = _diff_prompt(q, kb, vb, lam, subln, bp, seq, lambda_init)
            q, k, v = qkv[1]
            k_s.append(k)
            v_s.append(v)
            n_pool, page = cache_k.shape[1], cache_k.shape[2]
            cache_kt = jnp.transpose(cache_k, (0, 1, 3, 4, 5, 2)).reshape(-1, n_pool, d_attn, page)
            cache_vr = cache_v.reshape(-1, n_pool, page * (d_attn // DIFF_V_DIM), DIFF_V_DIM)
            o_s = _diff_sample(q, k, v, cache_kt, cache_vr, a, page_table, lam, subln, lambda_init)
            xp = _mm([o_p], [w_o_attn], m=n_p, n=d, layer=a, epilogue="res", e=src_p, e_off=off_p, name="attn_out")
            xs = _mm([o_s], [w_o_attn], m=bs, n=d, layer=a, epilogue="res", e=src_s, e_off=off_s, name="attn_out")

        g_cross = norm_cross[layer][None]
        mk = _mm([mem], [w_cross_k], m=bp * n_mem, n=d_cross, layer=layer, name="mem_k")
        mv = _mm([mem], [w_cross_v], m=bp * n_mem, n=d_cross, layer=layer, name="mem_v")
        memk_p.append(mk)
        memv_p.append(mv)
        q_p = _mm([xp], [w_cross_q], m=n_p, n=d_cross, layer=layer, prologue="norm", g=g_cross,
                  out_dtypes=(BF16,), name="cross_q")
        o_p = _cross_prompt(q_p, mk, mv, bp, seq)
        xp = _mm([o_p], [w_cross_o], m=n_p, n=d, layer=layer, epilogue="res", e=xp, name="cross_out")
        q_s = _mm([xs], [w_cross_q], m=bs, n=d_cross, layer=layer, prologue="norm", g=g_cross, name="cross_q")
        mem_rows = n_mem * (d_cross // CROSS_HEAD_DIM)
        o_s = _cross_sample(q_s, cache_mem_k.reshape(depth, bs, mem_rows, CROSS_HEAD_DIM),
                            cache_mem_v.reshape(depth, bs, mem_rows, CROSS_HEAD_DIM), layer)
        xs = _mm([o_s], [w_cross_o], m=bs, n=d, layer=layer, epilogue="res", e=xs, name="cross_out")

        last = layer == depth - 1
        out = _moe(xp, xs, layer, norm_ffn[layer][None], w_group, b_group, w_router, b_router,
                   w_exp_gate, w_exp_up, w_exp_down, norm_final[None], last)
        if last:
            xp, xs = out
        else:
            xc = out

    n_h = d_attn // DIFF_V_DIM
    n_ch = d_cross // CROSS_HEAD_DIM
    unpack = lambda hs, b: (jnp.stack([h[..., :SSM_STATE] for h in hs]), jnp.stack([h[..., SSM_STATE:] for h in hs]))
    s5re_p, s5im_p = unpack(s5_p, bp)
    s5re_s, s5im_s = unpack(s5_s, bs)
    return (xp.reshape(bp, seq, d), xs.reshape(bs, 1, d),
            jnp.stack(conv_p), jnp.stack(conv_s),
            s5re_p, s5im_p, s5re_s, s5im_s,
            jnp.stack(k_p).reshape(-1, bp, seq, n_h, 2, DIFF_HEAD_DIM),
            jnp.stack(v_p).reshape(-1, bp, seq, n_h, DIFF_V_DIM),
            jnp.stack(k_s).reshape(-1, bs, 1, n_h, 2, DIFF_HEAD_DIM),
            jnp.stack(v_s).reshape(-1, bs, 1, n_h, DIFF_V_DIM),
            jnp.stack(memk_p).reshape(depth, bp, n_mem, n_ch, CROSS_HEAD_DIM),
            jnp.stack(memv_p).reshape(depth, bp, n_mem, n_ch, CROSS_HEAD_DIM))
```
